```python
import math
import jax, jax.numpy as jnp
from jax import lax
import numpy as np

D_MODEL = 1024
BATCH = 8
SEQ = 2048
DEPTH = 4
DEC_BATCH = 32
DEC_SEQ = 4
PAST_LEN = 8192
PAGE_SIZE = 128

F32 = jnp.float32
EPS = 1e-6
D_PLE = 256
D_REC = D_MODEL // 2
REC_HEADS = 8
REC_HEAD_DIM = D_REC // REC_HEADS
CONV_REC = 4
LRU_C = 8.0
D_POOL = D_MODEL // 2
POOL_WINDOWS = (2, 4, 8, 16)
POOL_GROUPS = len(POOL_WINDOWS)
POOL_GROUP_DIM = D_POOL // POOL_GROUPS
POOL_BUF = max(POOL_WINDOWS) - 1
N_HEADS = 8
HEAD_DIM = 128
N_KV_HEADS = 4
KV_GROUP = N_HEADS // N_KV_HEADS
IDX_HEADS = 8
IDX_DIM = 64
TOPK_MAX = 256
ROPE_THETA = 10000.0
Q_BLOCK = 128
ATTN_IN = N_HEADS * HEAD_DIM + 2 * N_KV_HEADS * HEAD_DIM + IDX_HEADS * IDX_DIM + IDX_DIM + IDX_HEADS
D_FF = 11 * D_MODEL // 4
CONV_FF = 3
N_REC_LAYERS = (DEPTH + 1) // 2
N_ATTN_LAYERS = DEPTH // 2

kernel_name = 'hawk_pool_dsa_convffn_ple_step'


def rmsnorm(x, g):
    xf = x.astype(F32)
    y = xf * lax.rsqrt(jnp.mean(xf * xf, axis=-1, keepdims=True) + EPS)
    return (y * g.astype(F32)).astype(x.dtype)


def causal_dwconv(x, buf, w, b):
    width = w.shape[0]
    T = x.shape[1]
    xe = jnp.concatenate([buf.astype(x.dtype), x], axis=1)
    y = xe[:, 0:T] * w[0]
    for k in range(1, width):
        y = y + xe[:, k:k + T] * w[k]
    return y + b, xe[:, xe.shape[1] - (width - 1):]


def rope(x, pos):
    half = x.shape[-1] // 2
    inv = jnp.power(ROPE_THETA, -jnp.arange(half, dtype=F32) / half)
    ang = pos[:, None] * inv[None, :]
    cos = jnp.cos(ang)[None, :, None, :]
    sin = jnp.sin(ang)[None, :, None, :]
    xf = x.astype(F32)
    x1, x2 = xf[..., :half], xf[..., half:]
    return jnp.concatenate([x1 * cos - x2 * sin, x1 * sin + x2 * cos], axis=-1).astype(x.dtype)


def rg_lru(x, h0, w_r, b_r, w_i, b_i, lam):
    B, T, C = x.shape
    xh = x.reshape(B, T, REC_HEADS, REC_HEAD_DIM)
    r = jax.nn.sigmoid((jnp.einsum('bthi,hij->bthj', xh, w_r).reshape(B, T, C) + b_r).astype(F32))
    g_in = jax.nn.sigmoid((jnp.einsum('bthi,hij->bthj', xh, w_i).reshape(B, T, C) + b_i).astype(F32))
    log_a = -LRU_C * r * jax.nn.softplus(-lam.astype(F32))
    a = jnp.exp(log_a)
    u = jnp.sqrt(-jnp.expm1(2.0 * log_a)) * g_in * x.astype(F32)

    def step(h, au):
        h = au[0] * h + au[1]
        return h, h

    h_last, hs = lax.scan(step, h0.astype(F32), (jnp.swapaxes(a, 0, 1), jnp.swapaxes(u, 0, 1)))
    return jnp.swapaxes(hs, 0, 1).astype(x.dtype), h_last.astype(h0.dtype)


def multiscale_pool(x, buf, pos0, w_pool, scale):
    B, T, C = x.shape
    xe = jnp.concatenate([buf.astype(x.dtype), x], axis=1)
    xf = xe.astype(F32)
    csum = jnp.concatenate([jnp.zeros((B, 1, C), F32), jnp.cumsum(xf, axis=1)], axis=1)
    end = csum[:, POOL_BUF + 1:]
    pos = pos0 + jnp.arange(T, dtype=F32)
    means = []
    for g, w in enumerate(POOL_WINDOWS):
        sl = slice(g * POOL_GROUP_DIM, (g + 1) * POOL_GROUP_DIM)
        start = csum[:, POOL_BUF + 1 - w:POOL_BUF + 1 - w + T, sl]
        cnt = jnp.minimum(jnp.float32(w), pos + 1.0)[None, :, None]
        means.append((end[..., sl] - start) / cnt)
    d = jnp.concatenate(means, axis=-1) - xf[:, POOL_BUF:]
    d = d.reshape(B, T, POOL_GROUPS, POOL_GROUP_DIM)
    y = jnp.einsum('btgc,gcd->btgd', d, w_pool.astype(F32)).reshape(B, T, C) * scale.astype(F32)
    return y.astype(x.dtype), xe[:, xe.shape[1] - POOL_BUF:]


def rec_pool_mixer(h, pos0, conv_buf, h0, pool_buf, norm_g, w_in, conv_w, conv_b,
                   w_r, b_r, w_i, b_i, lam, w_pool, pool_scale, w_out):
    xn = rmsnorm(h, norm_g)
    z = xn @ w_in
    xa, ga, xb = z[..., :D_REC], z[..., D_REC:2 * D_REC], z[..., 2 * D_REC:]
    xa_c, new_conv = causal_dwconv(xa, conv_buf, conv_w, conv_b)
    ya, h_last = rg_lru(xa_c, h0, w_r, b_r, w_i, b_i, lam)
    ya = ya * jax.nn.gelu(ga)
    yb, new_pool = multiscale_pool(xb, pool_buf, pos0, w_pool, pool_scale)
    y = jnp.concatenate([ya, yb], axis=-1) @ w_out
    return h + y, new_conv, h_last, new_pool


def attn_project(h, pos0, norm_g, w_in, q_norm, k_norm):
    B, T, _ = h.shape
    xn = rmsnorm(h, norm_g)
    z = xn @ w_in
    o1 = N_HEADS * HEAD_DIM
    o2 = o1 + N_KV_HEADS * HEAD_DIM
    o3 = o2 + N_KV_HEADS * HEAD_DIM
    o4 = o3 + IDX_HEADS * IDX_DIM
    o5 = o4 + IDX_DIM
    q = z[..., :o1].reshape(B, T, N_HEADS, HEAD_DIM)
    k = z[..., o1:o2].reshape(B, T, N_KV_HEADS, HEAD_DIM)
    v = z[..., o2:o3].reshape(B, T, N_KV_HEADS, HEAD_DIM)
    qi = z[..., o3:o4].reshape(B, T, IDX_HEADS, IDX_DIM)
    ki = z[..., o4:o5]
    wi = z[..., o5:]
    pos = pos0 + jnp.arange(T, dtype=F32)
    q = rope(rmsnorm(q, q_norm), pos)
    k = rope(rmsnorm(k, k_norm), pos)
    qi = rope(qi, pos)
    ki = rope(ki[:, :, None, :], pos)[:, :, 0]
    return q, k, v, qi, ki, wi


def index_scores(qi, wi, ki):
    s = jnp.einsum('bqhd,bsd->bqhs', qi.astype(F32), ki.astype(F32)) * (IDX_DIM ** -0.5)
    return jnp.einsum('bqhs,bqh->bqs', jax.nn.relu(s), wi.astype(F32) * (IDX_HEADS ** -0.5))


def gather_rows(a, idx):
    return jax.vmap(lambda ab, ib: ab[ib])(a, idx)


def sparse_attend(q, k_sel, v_sel, valid):
    B, Q = q.shape[0], q.shape[1]
    qg = q.reshape(B, Q, N_KV_HEADS, KV_GROUP, HEAD_DIM).astype(F32)
    s = jnp.einsum('bqhgd,bqkhd->bqhgk', qg, k_sel.astype(F32)) * (HEAD_DIM ** -0.5)
    s = jnp.where(valid[:, :, None, None, :], s, -jnp.inf)
    p = jax.nn.softmax(s, axis=-1)
    o = jnp.einsum('bqhgk,bqkhd->bqhgd', p, v_sel.astype(F32))
    return o.reshape(B, Q, N_HEADS * HEAD_DIM).astype(q.dtype)


def dsa_prompt(q, k, v, qi, ki, wi):
    B, T = q.shape[0], q.shape[1]
    topk = min(TOPK_MAX, T // 4)
    kpos = jnp.arange(T)

    def block(q0):
        qb = lax.dynamic_slice_in_dim(q, q0, Q_BLOCK, axis=1)
        qib = lax.dynamic_slice_in_dim(qi, q0, Q_BLOCK, axis=1)
        wib = lax.dynamic_slice_in_dim(wi, q0, Q_BLOCK, axis=1)
        qpos = q0 + jnp.arange(Q_BLOCK)
        sc = index_scores(qib, wib, ki)
        sc = jnp.where((kpos[None, :] <= qpos[:, None])[None], sc, -jnp.inf)
        _, idx = lax.top_k(sc, topk)
        valid = idx <= qpos[None, :, None]
        return sparse_attend(qb, gather_rows(k, idx), gather_rows(v, idx), valid)

    o = lax.map(block, jnp.arange(0, T, Q_BLOCK))
    return jnp.swapaxes(o, 0, 1).reshape(B, T, N_HEADS * HEAD_DIM)


def dsa_sample(q, k, v, qi, ki, wi, cache_k, cache_v, cache_kidx, page_table, layer):
    B, T = q.shape[0], q.shape[1]
    past = page_table.shape[1] * PAGE_SIZE
    L = past + T
    topk = min(TOPK_MAX, L // 4)
    ki_past = cache_kidx[layer, page_table].reshape(B, past, IDX_DIM)
    ki_all = jnp.concatenate([ki_past.astype(ki.dtype), ki], axis=1)
    qpos = past + jnp.arange(T)
    sc = index_scores(qi, wi, ki_all)
    sc = jnp.where((jnp.arange(L)[None, :] <= qpos[:, None])[None], sc, -jnp.inf)
    _, idx = lax.top_k(sc, topk)
    valid = idx <= qpos[None, :, None]
    in_past = idx < past
    pidx = jnp.minimum(idx, past - 1)
    phys = jax.vmap(lambda pt, pg: pt[pg])(page_table, pidx // PAGE_SIZE)
    off = pidx % PAGE_SIZE
    nidx = jnp.clip(idx - past, 0, T - 1)
    sel = in_past[..., None, None]
    k_sel = jnp.where(sel, cache_k[layer, phys, off].astype(k.dtype), gather_rows(k, nidx))
    v_sel = jnp.where(sel, cache_v[layer, phys, off].astype(v.dtype), gather_rows(v, nidx))
    return sparse_attend(q, k_sel, v_sel, valid)


def conv_ffn(h, buf, norm_g, w_up, conv_w, conv_b, w_down):
    xn = rmsnorm(h, norm_g)
    z = xn @ w_up
    g, u = z[..., :D_FF], z[..., D_FF:]
    gc, new_buf = causal_dwconv(g, buf, conv_w, conv_b)
    return h + (jax.nn.gelu(gc) * u) @ w_down, new_buf


def ple_add(h, p, norm_g, w_p, w_g):
    gate = jax.nn.sigmoid((rmsnorm(h, norm_g) @ w_g).astype(F32)).astype(h.dtype)
    return h + (p @ w_p) * gate


def setup_inputs(seed: int = 0) -> dict:
    key = jax.random.key(seed)
    ks = iter(jax.random.split(key, 48))

    def nrm(shape, scale):
        return jax.random.normal(next(ks), shape, F32) * scale

    NR, NA = N_REC_LAYERS, N_ATTN_LAYERS
    n_pages = PAST_LEN // PAGE_SIZE
    n_used = DEC_BATCH * n_pages
    n_pool = n_used + n_used // 4
    x_prompt = nrm((BATCH, SEQ, D_MODEL), 1.0)
    x_sample = nrm((DEC_BATCH, DEC_SEQ, D_MODEL), 1.0)
    cache_k = nrm((NA, n_pool, PAGE_SIZE, N_KV_HEADS, HEAD_DIM), 1.0)
    cache_v = nrm((NA, n_pool, PAGE_SIZE, N_KV_HEADS, HEAD_DIM), 1.0)
    cache_kidx = nrm((NA, n_pool, PAGE_SIZE, IDX_DIM), 1.0)
    state_rec_conv = nrm((NR, DEC_BATCH, CONV_REC - 1, D_REC), 1.0)
    state_rec_h = nrm((NR, DEC_BATCH, D_REC), 0.5)
    state_pool = nrm((NR, DEC_BATCH, POOL_BUF, D_POOL), 1.0)
    state_ffn_conv = nrm((DEPTH, DEC_BATCH, CONV_FF - 1, D_FF), 1.0)
    page_table = jax.random.permutation(next(ks), n_pool)[:n_used].reshape(DEC_BATCH, n_pages).astype(jnp.int32)
    p_prompt = nrm((DEPTH, BATCH, SEQ, D_PLE), 1.0)
    p_sample = nrm((DEPTH, DEC_BATCH, DEC_SEQ, D_PLE), 1.0)
    norm_mix = 1.0 + nrm((DEPTH, D_MODEL), 0.05)
    norm_ffn = 1.0 + nrm((DEPTH, D_MODEL), 0.05)
    norm_ple = 1.0 + nrm((DEPTH, D_MODEL), 0.05)
    w_in_rec = nrm((NR, D_MODEL, 2 * D_REC + D_POOL), D_MODEL ** -0.5)
    conv_rec_w = nrm((NR, CONV_REC, D_REC), CONV_REC ** -0.5)
    conv_rec_b = nrm((NR, D_REC), 0.02)
    w_rgate = nrm((NR, REC_HEADS, REC_HEAD_DIM, REC_HEAD_DIM), REC_HEAD_DIM ** -0.5)
    b_rgate = nrm((NR, D_REC), 0.02)
    w_igate = nrm((NR, REC_HEADS, REC_HEAD_DIM, REC_HEAD_DIM), REC_HEAD_DIM ** -0.5)
    b_igate = nrm((NR, D_REC), 0.02)
    a_c = jax.random.uniform(next(ks), (NR, D_REC), F32, 0.9, 0.999)
    s = a_c ** (1.0 / LRU_C)
    lru_lambda = jnp.log(s) - jnp.log1p(-s)
    w_pool = nrm((NR, POOL_GROUPS, POOL_GROUP_DIM, POOL_GROUP_DIM), POOL_GROUP_DIM ** -0.5)
    pool_scale = 1.0 + nrm((NR, D_POOL), 0.05)
    w_out_rec = nrm((NR, D_REC + D_POOL, D_MODEL), (D_REC + D_POOL) ** -0.5)
    w_in_attn = nrm((NA, D_MODEL, ATTN_IN), D_MODEL ** -0.5)
    q_norm = 1.0 + nrm((NA, HEAD_DIM), 0.05)
    k_norm = 1.0 + nrm((NA, HEAD_DIM), 0.05)
    w_out_attn = nrm((NA, N_HEADS * HEAD_DIM, D_MODEL), (N_HEADS * HEAD_DIM) ** -0.5)
    w_up = nrm((DEPTH, D_MODEL, 2 * D_FF), D_MODEL ** -0.5)
    conv_ff_w = nrm((DEPTH, CONV_FF, D_FF), CONV_FF ** -0.5)
    conv_ff_b = nrm((DEPTH, D_FF), 0.02)
    w_down = nrm((DEPTH, D_FF, D_MODEL), D_FF ** -0.5)
    w_ple = nrm((DEPTH, D_PLE, D_MODEL), D_PLE ** -0.5)
    w_ple_gate = nrm((DEPTH, D_MODEL, D_MODEL), D_MODEL ** -0.5)
    return {'x_prompt': x_prompt, 'x_sample': x_sample,
            'cache_k': cache_k, 'cache_v': cache_v, 'cache_kidx': cache_kidx,
            'state_rec_conv': state_rec_conv, 'state_rec_h': state_rec_h,
            'state_pool': state_pool, 'state_ffn_conv': state_ffn_conv,
            'page_table': page_table, 'p_prompt': p_prompt, 'p_sample': p_sample,
            'norm_mix': norm_mix, 'norm_ffn': norm_ffn, 'norm_ple': norm_ple,
            'w_in_rec': w_in_rec, 'conv_rec_w': conv_rec_w, 'conv_rec_b': conv_rec_b,
            'w_rgate': w_rgate, 'b_rgate': b_rgate, 'w_igate': w_igate, 'b_igate': b_igate,
            'lru_lambda': lru_lambda, 'w_pool': w_pool, 'pool_scale': pool_scale, 'w_out_rec': w_out_rec,
            'w_in_attn': w_in_attn, 'q_norm': q_norm, 'k_norm': k_norm, 'w_out_attn': w_out_attn,
            'w_up': w_up, 'conv_ff_w': conv_ff_w, 'conv_ff_b': conv_ff_b, 'w_down': w_down,
            'w_ple': w_ple, 'w_ple_gate': w_ple_gate}


def reference(x_prompt, x_sample, cache_k, cache_v, cache_kidx, state_rec_conv, state_rec_h,
              state_pool, state_ffn_conv, page_table, p_prompt, p_sample,
              norm_mix, norm_ffn, norm_ple, w_in_rec, conv_rec_w, conv_rec_b,
              w_rgate, b_rgate, w_igate, b_igate, lru_lambda, w_pool, pool_scale, w_out_rec,
              w_in_attn, q_norm, k_norm, w_out_attn, w_up, conv_ff_w, conv_ff_b, w_down,
              w_ple, w_ple_gate):
    B = x_prompt.shape[0]
    past = page_table.shape[1] * PAGE_SIZE
    dt = x_prompt.dtype
    hp, hs = x_prompt, x_sample
    rc_p, rc_s, rh_p, rh_s, pl_p, pl_s = [], [], [], [], [], []
    k_p, k_s, v_p, v_s, ki_p, ki_s = [], [], [], [], [], []
    fc_p, fc_s = [], []
    for i in range(DEPTH):
        j = i // 2
        if i % 2 == 0:
            wts = (norm_mix[i], w_in_rec[j], conv_rec_w[j], conv_rec_b[j], w_rgate[j], b_rgate[j],
                   w_igate[j], b_igate[j], lru_lambda[j], w_pool[j], pool_scale[j], w_out_rec[j])
            hp, c, hh, pb = rec_pool_mixer(hp, 0, jnp.zeros((B, CONV_REC - 1, D_REC), dt),
                                           jnp.zeros((B, D_REC), dt),
                                           jnp.zeros((B, POOL_BUF, D_POOL), dt), *wts)
            rc_p.append(c)
            rh_p.append(hh)
            pl_p.append(pb)
            hs, c, hh, pb = rec_pool_mixer(hs, past, state_rec_conv[j], state_rec_h[j], state_pool[j], *wts)
            rc_s.append(c)
            rh_s.append(hh)
            pl_s.append(pb)
        else:
            q, k, v, qi, ki, wi = attn_project(hp, 0, norm_mix[i], w_in_attn[j], q_norm[j], k_norm[j])
            hp = hp + dsa_prompt(q, k, v, qi, ki, wi) @ w_out_attn[j]
            k_p.append(k)
            v_p.append(v)
            ki_p.append(ki)
            q, k, v, qi, ki, wi = attn_project(hs, past, norm_mix[i], w_in_attn[j], q_norm[j], k_norm[j])
            hs = hs + dsa_sample(q, k, v, qi, ki, wi, cache_k, cache_v, cache_kidx, page_table, j) @ w_out_attn[j]
            k_s.append(k)
            v_s.append(v)
            ki_s.append(ki)
        fw = (norm_ffn[i], w_up[i], conv_ff_w[i], conv_ff_b[i], w_down[i])
        hp, fb = conv_ffn(hp, jnp.zeros((B, CONV_FF - 1, D_FF), dt), *fw)
        fc_p.append(fb)
        hs, fb = conv_ffn(hs, state_ffn_conv[i], *fw)
        fc_s.append(fb)
        hp = ple_add(hp, p_prompt[i], norm_ple[i], w_ple[i], w_ple_gate[i])
        hs = ple_add(hs, p_sample[i], norm_ple[i], w_ple[i], w_ple_gate[i])
    return (hp, hs,
            jnp.stack(rc_p), jnp.stack(rc_s), jnp.stack(rh_p), jnp.stack(rh_s),
            jnp.stack(pl_p), jnp.stack(pl_s),
            jnp.stack(k_p), jnp.stack(k_s), jnp.stack(v_p), jnp.stack(v_s),
            jnp.stack(ki_p), jnp.stack(ki_s),
            jnp.stack(fc_p), jnp.stack(fc_s))
```

```python
import functools
import math

import jax
import jax.numpy as jnp
from jax import lax
from jax.experimental import pallas as pl
from jax.experimental.pallas import tpu as pltpu

F32 = jnp.float32
BF16 = jnp.bfloat16
EPS = 1e-6
LRU_C = 8.0
POOL_WINDOWS = (2, 4, 8, 16)
REC_HEADS = 8
CONV_REC = 4
CONV_FF = 3
POOL_BUF = max(POOL_WINDOWS) - 1
N_HEADS = 8
HEAD_DIM = 128
N_KV_HEADS = 4
IDX_HEADS = 8
IDX_DIM = 64
TOPK_MAX = 256
ROPE_THETA = 10000.0
PAGE_SIZE = 128

LANES = 128
SUBLANES = 8
VMEM_LIMIT = 56 * 1024 * 1024

HIST_CONV = SUBLANES
HIST_POOL = 2 * SUBLANES
HIST_FFN = SUBLANES


def _rms(x, g):
    return x * lax.rsqrt(jnp.mean(x * x, axis=-1, keepdims=True) + EPS) * g


def _dot(a, b):
    return jnp.dot(a.astype(BF16), b.astype(BF16), preferred_element_type=F32)


def _dot_nt(a, b):
    return lax.dot_general(a.astype(BF16), b.astype(BF16), (((1,), (1,)), ((), ())),
                           preferred_element_type=F32)


def _softplus(x):
    return jnp.maximum(x, 0.0) + jnp.log1p(jnp.exp(-jnp.abs(x)))


def _expm1(x):
    u = jnp.exp(x)
    near = (u - 1.0) * x / jnp.log(u)
    return jnp.where(u == 1.0, x, jnp.where(jnp.abs(x) > 0.5, u - 1.0, near))


def _tix(ta, start, size):
    t = slice(start, start + size) if isinstance(start, int) else pl.ds(start, size)
    return (slice(None), t, slice(None)) if ta == 1 else (t, slice(None), slice(None))


def _const_spec(shape):
    nd = len(shape)
    return pl.BlockSpec(shape, lambda *_: (0,) * nd, pipeline_mode=pl.Buffered(1))


def _tile_spec(shape, ta):
    if ta == 1:
        return pl.BlockSpec(shape, lambda i: (0, i, 0))
    return pl.BlockSpec(shape, lambda i: (i, 0, 0))


def _rec_body(ta, pos0, h_ref, g_ref, win_ref, cw_ref, cb_ref, wr_ref, br_ref, wi_ref, bi_ref,
              lam_ref, wp_ref, ps_ref, wout_ref, conv0_ref, h0_ref, pool0_ref,
              out_ref, conv_out_ref, hlast_ref, pool_out_ref,
              xa_ext, xb_ext, a_s, u_s, hs_s, hc_s):
    step = pl.program_id(0)
    A, S, D = h_ref.shape
    R = A * S
    n_t = S if ta == 1 else A
    C = xa_ext.shape[-1]

    @pl.when(step == 0)
    def _():
        xa_ext[_tix(ta, 0, HIST_CONV)] = conv0_ref[...]
        xb_ext[_tix(ta, 0, HIST_POOL)] = pool0_ref[...]
        hc_s[...] = h0_ref[...]

    x = h_ref[...].reshape(R, D)
    z = _dot(_rms(x, g_ref[...]), win_ref[...])
    ga = z[:, C:2 * C]
    xa_ext[_tix(ta, HIST_CONV, n_t)] = z[:, :C].reshape(A, S, C)
    xb_ext[_tix(ta, HIST_POOL, n_t)] = z[:, 2 * C:].reshape(A, S, C)

    xc = cb_ref[...][None]
    for k in range(CONV_REC):
        xc = xc + xa_ext[_tix(ta, HIST_CONV - (CONV_REC - 1) + k, n_t)] * cw_ref[k:k + 1, :][None]
    xc = xc.reshape(R, C)

    r = jax.nn.sigmoid(_dot(xc, wr_ref[...]) + br_ref[...])
    gi = jax.nn.sigmoid(_dot(xc, wi_ref[...]) + bi_ref[...])
    log_a = (-LRU_C * r) * _softplus(-lam_ref[...])
    a_s[...] = jnp.exp(log_a).reshape(A, S, C)
    u_s[...] = (jnp.sqrt(-_expm1(2.0 * log_a)) * gi * xc).reshape(A, S, C)

    def scan_step(t, h):
        ix = _tix(ta, t, 1)
        h = a_s[ix] * h + u_s[ix]
        hs_s[ix] = h
        return h

    if n_t <= SUBLANES:
        h = hc_s[...]
        for t in range(n_t):
            h = scan_step(t, h)
    else:
        h = lax.fori_loop(0, n_t, scan_step, hc_s[...], unroll=SUBLANES)
    hc_s[...] = h
    hlast_ref[...] = h
    ya = hs_s[...].reshape(R, C) * jax.nn.gelu(ga)

    G = C // len(POOL_WINDOWS)
    pos = pos0 + step * n_t + lax.broadcasted_iota(jnp.int32, (A, S, G), ta)
    posf = pos.astype(F32) + 1.0
    ds = []
    for gidx, w in enumerate(POOL_WINDOWS):
        ln = slice(gidx * G, (gidx + 1) * G)
        cur = xb_ext[_tix(ta, HIST_POOL, n_t)[:2] + (ln,)]
        acc = cur
        for j in range(1, w):
            acc = acc + xb_ext[_tix(ta, HIST_POOL - j, n_t)[:2] + (ln,)]
        ds.append(acc / jnp.minimum(jnp.float32(w), posf) - cur)
    d = jnp.concatenate(ds, axis=-1).reshape(R, C)
    yb = _dot(d, wp_ref[...]) * ps_ref[...]

    y = _dot(ya, wout_ref[:C, :]) + _dot(yb, wout_ref[C:, :])
    out_ref[...] = (x + y).reshape(A, S, D)

    new_conv = xa_ext[_tix(ta, n_t, HIST_CONV)]
    xa_ext[_tix(ta, 0, HIST_CONV)] = new_conv
    conv_out_ref[...] = new_conv
    new_pool = xb_ext[_tix(ta, n_t, HIST_POOL)]
    xb_ext[_tix(ta, 0, HIST_POOL)] = new_pool
    pool_out_ref[...] = new_pool


def _block_diag(w):
    H, a, b = w.shape
    eye = jnp.eye(H, dtype=w.dtype)
    return (eye[:, None, :, None] * w[:, :, None, :]).reshape(H * a, H * b)


def _rec_layer(h, ta, t_tile, pos0, conv0, h0, pool0, norm_g, w_in, conv_w, conv_b,
               w_r, b_r, w_i, b_i, lam, w_pool, pool_scale, w_out):
    D = h.shape[-1]
    C = conv_w.shape[-1]
    n_time = h.shape[ta]
    n_batch = h.shape[1 - ta]
    assert n_time % t_tile == 0
    if ta == 1:
        blk = (n_batch, t_tile, D)
        ext = lambda hist: (n_batch, hist + t_tile, C)
        hist_shape = lambda hist: (n_batch, hist, C)
        row_shape = (n_batch, 1, C)
        tile_c = (n_batch, t_tile, C)
    else:
        blk = (t_tile, n_batch, D)
        ext = lambda hist: (hist + t_tile, n_batch, C)
        hist_shape = lambda hist: (hist, n_batch, C)
        row_shape = (1, n_batch, C)
        tile_c = (t_tile, n_batch, C)
    row = lambda v: v.reshape(1, -1)
    args = (h, row(norm_g), w_in.astype(BF16), conv_w, row(conv_b),
            _block_diag(w_r).astype(BF16), row(b_r), _block_diag(w_i).astype(BF16), row(b_i),
            row(lam), _block_diag(w_pool).astype(BF16), row(pool_scale), w_out.astype(BF16),
            conv0, h0, pool0)
    in_specs = [_tile_spec(blk, ta)] + [_const_spec(a.shape) for a in args[1:]]
    out_shape = (jax.ShapeDtypeStruct(h.shape, F32),
                 jax.ShapeDtypeStruct(hist_shape(HIST_CONV), F32),
                 jax.ShapeDtypeStruct(row_shape, F32),
                 jax.ShapeDtypeStruct(hist_shape(HIST_POOL), F32))
    out_specs = (_tile_spec(blk, ta),
                 pl.BlockSpec(hist_shape(HIST_CONV), lambda i: (0, 0, 0)),
                 pl.BlockSpec(row_shape, lambda i: (0, 0, 0)),
                 pl.BlockSpec(hist_shape(HIST_POOL), lambda i: (0, 0, 0)))
    scratch = [pltpu.VMEM(ext(HIST_CONV), F32), pltpu.VMEM(ext(HIST_POOL), F32),
               pltpu.VMEM(tile_c, F32), pltpu.VMEM(tile_c, F32), pltpu.VMEM(tile_c, F32),
               pltpu.VMEM(row_shape, F32)]
    return pl.pallas_call(
        functools.partial(_rec_body, ta, pos0),
        grid=(n_time // t_tile,),
        in_specs=in_specs, out_specs=out_specs, out_shape=out_shape, scratch_shapes=scratch,
        compiler_params=pltpu.CompilerParams(dimension_semantics=("arbitrary",),
                                             vmem_limit_bytes=VMEM_LIMIT),
        name="rec_pool_mixer",
    )(*args)


def _ffn_body(ta, n_chunks, h_ref, p_ref, gf_ref, wup_ref, cw_ref, cb_ref, wdn_ref, gp_ref,
              wple_ref, wpg_ref, buf0_ref, out_ref, buf_out_ref, g_ext):
    step = pl.program_id(0)
    A, S, D = h_ref.shape
    R = A * S
    n_t = S if ta == 1 else A
    FF = g_ext.shape[-1]
    ch = FF // n_chunks

    @pl.when(step == 0)
    def _():
        g_ext[_tix(ta, 0, HIST_FFN)] = buf0_ref[...]

    x = h_ref[...].reshape(R, D)
    xn = _rms(x, gf_ref[...]).astype(BF16)
    acc = jnp.zeros((R, D), F32)
    for c in range(n_chunks):
        ln = slice(c * ch, (c + 1) * ch)
        g = jnp.dot(xn, wup_ref[:, ln], preferred_element_type=F32)
        u = jnp.dot(xn, wup_ref[:, FF + c * ch:FF + (c + 1) * ch], preferred_element_type=F32)
        g_ext[_tix(ta, HIST_FFN, n_t)[:2] + (ln,)] = g.reshape(A, S, ch)
        gc = cb_ref[:, ln][None]
        for k in range(CONV_FF):
            gc = gc + (g_ext[_tix(ta, HIST_FFN - (CONV_FF - 1) + k, n_t)[:2] + (ln,)]
                       * cw_ref[k:k + 1, ln][None])
        act = jax.nn.gelu(gc).reshape(R, ch) * u
        acc = acc + _dot(act, wdn_ref[ln, :])
    h1 = x + acc
    gate = jax.nn.sigmoid(_dot(_rms(h1, gp_ref[...]), wpg_ref[...]))
    pp = _dot(p_ref[...].reshape(R, p_ref.shape[-1]), wple_ref[...])
    out_ref[...] = (h1 + pp * gate).reshape(A, S, D)

    new_buf = g_ext[_tix(ta, n_t, HIST_FFN)]
    g_ext[_tix(ta, 0, HIST_FFN)] = new_buf
    buf_out_ref[...] = new_buf


def _ffn_layer(h, p, ta, t_tile, buf0, norm_f, w_up, conv_w, conv_b, w_down, norm_p, w_ple, w_pg,
               n_chunks=2):
    D = h.shape[-1]
    FF = conv_w.shape[-1]
    n_time = h.shape[ta]
    n_batch = h.shape[1 - ta]
    assert n_time % t_tile == 0 and FF % (n_chunks * LANES) == 0
    if ta == 1:
        blk = lambda c: (n_batch, t_tile, c)
        ext = (n_batch, HIST_FFN + t_tile, FF)
        hist = (n_batch, HIST_FFN, FF)
    else:
        blk = lambda c: (t_tile, n_batch, c)
        ext = (HIST_FFN + t_tile, n_batch, FF)
        hist = (HIST_FFN, n_batch, FF)
    row = lambda v: v.reshape(1, -1)
    args = (h, p, row(norm_f), w_up.astype(BF16), conv_w, row(conv_b), w_down.astype(BF16),
            row(norm_p), w_ple.astype(BF16), w_pg.astype(BF16), buf0)
    in_specs = ([_tile_spec(blk(D), ta), _tile_spec(blk(p.shape[-1]), ta)]
                + [_const_spec(a.shape) for a in args[2:]])
    out_shape = (jax.ShapeDtypeStruct(h.shape, F32), jax.ShapeDtypeStruct(hist, F32))
    out_specs = (_tile_spec(blk(D), ta), pl.BlockSpec(hist, lambda i: (0, 0, 0)))
    return pl.pallas_call(
        functools.partial(_ffn_body, ta, n_chunks),
        grid=(n_time // t_tile,),
        in_specs=in_specs, out_specs=out_specs, out_shape=out_shape,
        scratch_shapes=[pltpu.VMEM(ext, F32)],
        compiler_params=pltpu.CompilerParams(dimension_semantics=("arbitrary",),
                                             vmem_limit_bytes=VMEM_LIMIT),
        name="conv_ffn_ple",
    )(*args)


def _pad_hist(state, hist, ta):
    n = state.shape[1]
    if ta == 1:
        return jnp.pad(state, ((0, 0), (hist - n, 0), (0, 0)))
    return jnp.pad(jnp.swapaxes(state, 0, 1), ((hist - n, 0), (0, 0), (0, 0)))


def _unpad_hist(block, n, ta):
    if ta == 1:
        return block[:, block.shape[1] - n:, :]
    return jnp.swapaxes(block[block.shape[0] - n:], 0, 1)


Q_OFF = 0
K_OFF = N_HEADS * HEAD_DIM
V_OFF = K_OFF + N_KV_HEADS * HEAD_DIM
QI_OFF = V_OFF + N_KV_HEADS * HEAD_DIM
KW_OFF = QI_OFF + IDX_HEADS * IDX_DIM
ATTN_COLS = KW_OFF + LANES
IDX_SCALE = (IDX_DIM ** -0.5) * (IDX_HEADS ** -0.5)
ATTN_SCALE = HEAD_DIM ** -0.5


def _rope_tables(pos, dim):
    half = dim // 2
    inv = jnp.power(ROPE_THETA, -jnp.arange(half, dtype=F32) / half)
    ang = pos.astype(F32)[:, None] * inv[None, :]
    cos, sin = jnp.cos(ang), jnp.sin(ang)
    reps = LANES // dim
    return (jnp.tile(jnp.concatenate([cos, cos], axis=-1), (1, reps)),
            jnp.tile(jnp.concatenate([-sin, sin], axis=-1), (1, reps)))


def _proj_body(x_ref, g_ref, w_ref, qn_ref, kn_ref, cq_ref, sq_ref, ci_ref, si_ref,
               q_ref, k_ref, v_ref, qi_ref, kw_ref):
    z = _dot(_rms(x_ref[...], g_ref[...]), w_ref[...])
    cq, sq, ci, si = cq_ref[...], sq_ref[...], ci_ref[...], si_ref[...]
    lane = lax.broadcasted_iota(jnp.int32, cq.shape, 1)
    low_half = (lane % IDX_DIM) < (IDX_DIM // 2)

    def rope_head(t):
        return t * cq + pltpu.roll(t, HEAD_DIM // 2, 1) * sq

    def rope_idx(t):
        partner = jnp.where(low_half, pltpu.roll(t, LANES - IDX_DIM // 2, 1), pltpu.roll(t, IDX_DIM // 2, 1))
        return t * ci + partner * si

    def head_norm(t, g):
        return t * lax.rsqrt(jnp.mean(t * t, axis=-1, keepdims=True) + EPS) * g

    for h in range(N_HEADS):
        ln = slice(h * HEAD_DIM, (h + 1) * HEAD_DIM)
        q_ref[:, ln] = rope_head(head_norm(z[:, ln], qn_ref[...])) * ATTN_SCALE
    for h in range(N_KV_HEADS):
        ln = slice(h * HEAD_DIM, (h + 1) * HEAD_DIM)
        k_ref[:, ln] = rope_head(head_norm(z[:, K_OFF + h * HEAD_DIM:K_OFF + (h + 1) * HEAD_DIM], kn_ref[...]))
    v_ref[...] = z[:, V_OFF:QI_OFF]
    for s in range(IDX_HEADS * IDX_DIM // LANES):
        ln = slice(s * LANES, (s + 1) * LANES)
        qi_ref[:, ln] = rope_idx(z[:, QI_OFF + s * LANES:QI_OFF + (s + 1) * LANES])
    kw = z[:, KW_OFF:]
    kw_ref[...] = jnp.where(lane < IDX_DIM, rope_idx(kw), kw * IDX_SCALE)


def _attn_project(x, pos_tab, r_tile, norm_g, w_in, q_norm, k_norm):
    N, D = x.shape
    P = pos_tab.shape[0]
    assert N % r_tile == 0 and P % r_tile == 0
    w = jnp.pad(w_in, ((0, 0), (0, ATTN_COLS - w_in.shape[1]))).astype(BF16)
    cq, sq = _rope_tables(pos_tab, HEAD_DIM)
    ci, si = _rope_tables(pos_tab, IDX_DIM)
    n_tab = P // r_tile
    row_spec = lambda c: pl.BlockSpec((r_tile, c), lambda i: (i, 0))
    tab_spec = pl.BlockSpec((r_tile, LANES), lambda i: (i % n_tab, 0))
    row = lambda v: v.reshape(1, -1)
    widths = (N_HEADS * HEAD_DIM, N_KV_HEADS * HEAD_DIM, N_KV_HEADS * HEAD_DIM, IDX_HEADS * IDX_DIM, LANES)
    return pl.pallas_call(
        _proj_body,
        grid=(N // r_tile,),
        in_specs=[row_spec(D), _const_spec((1, D)), _const_spec(w.shape), _const_spec((1, HEAD_DIM)),
                  _const_spec((1, HEAD_DIM)), tab_spec, tab_spec, tab_spec, tab_spec],
        out_specs=tuple(row_spec(c) for c in widths),
        out_shape=tuple(jax.ShapeDtypeStruct((N, c), F32) for c in widths),
        compiler_params=pltpu.CompilerParams(dimension_semantics=("arbitrary",),
                                             vmem_limit_bytes=VMEM_LIMIT),
        name="attn_project",
    )(x, row(norm_g), w, row(q_norm), row(k_norm), cq, sq, ci, si)


def _sort_key(x):
    bits = lax.bitcast_convert_type(jnp.where(x == 0.0, 0.0, x), jnp.int32)
    return bits ^ (jnp.right_shift(bits, 31) & jnp.int32(0x7FFFFFFF))


def _topk_select(segs, k, tri_ref):
    keys = [_sort_key(jnp.where(valid, x, -jnp.inf)) for x, valid in segs]
    kf = jnp.float32(k)

    def count(pred_fn):
        tot = None
        for key in keys:
            c = jnp.sum(jnp.where(pred_fn(key), 1.0, 0.0), axis=-1, keepdims=True)
            tot = c if tot is None else tot + c
        return tot

    int_min = jnp.int32(-2 ** 31)
    r = jnp.where(count(lambda key: key >= 0) >= kf, jnp.int32(0), int_min)

    def bit_step(i, r):
        cand = r | jnp.left_shift(jnp.int32(1), 30 - i)
        return jnp.where(count(lambda key: key >= cand) >= kf, cand, r)

    r = lax.fori_loop(0, 31, bit_step, r)
    need = kf - count(lambda key: key > r)
    off = jnp.zeros_like(need)
    out = []
    for key, (_, valid) in zip(keys, segs):
        n = key.shape[-1]
        w = min(n, LANES)
        pieces = []
        for c in range(n // w):
            kc = key[:, c * w:(c + 1) * w]
            eq = jnp.where(kc == r, 1.0, 0.0)
            pref = jnp.dot(eq.astype(BF16), tri_ref[:w, :w], preferred_element_type=F32)
            take_tie = (off + pref <= need) & (kc == r)
            pieces.append((kc > r) | take_tie)
            off = off + pref[:, w - 1:w]
        sel = pieces[0] if len(pieces) == 1 else jnp.concatenate(pieces, axis=-1)
        out.append(sel & valid)
    return out


def _index_scores(qi, wq, ki):
    kb = ki.astype(BF16)
    tot = None
    for h in range(IDX_HEADS):
        s = _dot_nt(qi[:, h * IDX_DIM:(h + 1) * IDX_DIM], kb)
        t = jnp.maximum(s, 0.0) * wq[:, IDX_DIM + h:IDX_DIM + h + 1]
        tot = t if tot is None else tot + t
    return tot


def _dsa_prompt_body(topk, q_ref, qi_ref, kwq_ref, k_ref, v_ref, kwk_ref, h_ref, wout_ref, tri_ref, out_ref):
    j = pl.program_id(1)
    Tq = q_ref.shape[0]
    S = k_ref.shape[0]
    qpos = j * Tq + lax.broadcasted_iota(jnp.int32, (Tq, S), 0)
    kpos = lax.broadcasted_iota(jnp.int32, (Tq, S), 1)
    causal = kpos <= qpos
    scores = _index_scores(qi_ref[...], kwq_ref[...], kwk_ref[:, :IDX_DIM])
    (sel,) = _topk_select([(scores, causal)], topk, tri_ref)
    bias = jnp.where(sel, 0.0, -jnp.inf)
    outs = []
    for g in range(N_KV_HEADS):
        ln = slice(g * HEAD_DIM, (g + 1) * HEAD_DIM)
        kg = k_ref[:, ln].astype(BF16)
        vg = v_ref[:, ln].astype(BF16)
        for hh in range(N_HEADS // N_KV_HEADS):
            hd = g * (N_HEADS // N_KV_HEADS) + hh
            s = _dot_nt(q_ref[:, hd * HEAD_DIM:(hd + 1) * HEAD_DIM], kg) + bias
            p = jnp.exp(s - jnp.max(s, axis=-1, keepdims=True))
            outs.append(_dot(p, vg) / jnp.sum(p, axis=-1, keepdims=True))
    o = jnp.concatenate(outs, axis=-1)
    out_ref[...] = h_ref[...] + _dot(o, wout_ref[...])


def _tri(n):
    i = jnp.arange(n)
    return (i[:, None] <= i[None, :]).astype(BF16)


def _dsa_prompt(h, q, k, v, qi, kw, w_out, q_tile):
    B, T, D = h.shape
    topk = min(TOPK_MAX, T // 4)
    qspec = lambda c: pl.BlockSpec((None, q_tile, c), lambda b, j: (b, j, 0))
    kspec = lambda c: pl.BlockSpec((None, T, c), lambda b, j: (b, 0, 0))
    return pl.pallas_call(
        functools.partial(_dsa_prompt_body, topk),
        grid=(B, T // q_tile),
        in_specs=[qspec(q.shape[-1]), qspec(qi.shape[-1]), qspec(LANES), kspec(k.shape[-1]), kspec(v.shape[-1]),
                  kspec(LANES), qspec(D), _const_spec(w_out.shape), _const_spec((LANES, LANES))],
        out_specs=qspec(D),
        out_shape=jax.ShapeDtypeStruct(h.shape, F32),
        compiler_params=pltpu.CompilerParams(dimension_semantics=("arbitrary", "arbitrary"),
                                             vmem_limit_bytes=VMEM_LIMIT),
        name="dsa_prompt",
    )(q, qi, kw, k, v, kw, h, w_out.astype(BF16), _tri(LANES))


PAGES_PER_CHUNK = 16
Q_ROWS = SUBLANES


def _dsa_sample_body(layer, topk, n_pages, pt_ref,
                     q8_ref, qi_ref, wq_ref, knew_ref, vnew_ref, kwnew_ref, h_ref, wout_ref, tri_ref,
                     ckidx_ref, ck_ref, cv_ref, out_ref, kidx_buf, kv_buf, sc_ref, sems):
    b = pl.program_id(0)
    n_chunks = n_pages // PAGES_PER_CHUNK
    n_loads = 2 * n_chunks
    past = n_pages * PAGE_SIZE
    page_rows = PAGE_SIZE * N_KV_HEADS
    ch_keys = PAGES_PER_CHUNK * PAGE_SIZE
    n_tok = h_ref.shape[0]

    def kidx_copy(p):
        return pltpu.make_async_copy(ckidx_ref.at[layer, pt_ref[b, p]], kidx_buf.at[p], sems.at[0])

    def kv_copy(i, p):
        src = ck_ref if i < n_chunks else cv_ref
        page = pt_ref[b, (i % n_chunks) * PAGES_PER_CHUNK + p]
        return pltpu.make_async_copy(src.at[layer, page], kv_buf.at[i % 2, pl.ds(p * page_rows, page_rows)],
                                     sems.at[1 + i % 2])

    def start_load(i):
        for p in range(PAGES_PER_CHUNK):
            kv_copy(i, p).start()

    def wait_load(i):
        for p in range(PAGES_PER_CHUNK):
            kv_copy(i, p).wait()

    def head_rows(slot, g):
        return kv_buf[slot, pl.ds(g, ch_keys, stride=N_KV_HEADS), :]

    for p in range(n_pages):
        kidx_copy(p).start()
    start_load(0)
    start_load(1)
    for p in range(n_pages):
        kidx_copy(p).wait()

    qi = qi_ref[...]
    wq = wq_ref[...]

    def head_sum(s):
        n = s.shape[-1]
        return jnp.sum((jnp.maximum(s, 0.0) * wq).reshape(Q_ROWS, IDX_HEADS, n), axis=1)

    i_past = head_sum(_dot_nt(qi, kidx_buf[...].reshape(past, IDX_DIM)))
    i_new = head_sum(_dot_nt(qi, kwnew_ref[:, :IDX_DIM]))
    tq = lax.broadcasted_iota(jnp.int32, i_new.shape, 0) % n_tok
    valid_new = lax.broadcasted_iota(jnp.int32, i_new.shape, 1) <= tq
    valid_past = lax.broadcasted_iota(jnp.int32, i_past.shape, 1) >= 0
    sel_past, sel_new = _topk_select([(i_past, valid_past), (i_new, valid_new)], topk, tri_ref)
    bias_past = jnp.where(sel_past, 0.0, -jnp.inf)
    bias_new = jnp.where(sel_new, 0.0, -jnp.inf)

    for c in range(n_chunks):
        wait_load(c)
        ks = slice(c * ch_keys, (c + 1) * ch_keys)
        for g in range(N_KV_HEADS):
            sc_ref[g, :, ks] = _dot_nt(q8_ref[g], head_rows(c % 2, g)) + bias_past[:, ks]
        if c + 2 < n_loads:
            start_load(c + 2)

    acc, denom = [], []
    for g in range(N_KV_HEADS):
        ln = slice(g * HEAD_DIM, (g + 1) * HEAD_DIM)
        s_new = _dot_nt(q8_ref[g], knew_ref[:, ln]) + bias_new
        s_past = sc_ref[g]
        m = jnp.maximum(jnp.max(s_past, axis=-1, keepdims=True), jnp.max(s_new, axis=-1, keepdims=True))
        p_past = jnp.exp(s_past - m)
        p_new = jnp.exp(s_new - m)
        sc_ref[g] = p_past
        denom.append(jnp.sum(p_past, axis=-1, keepdims=True) + jnp.sum(p_new, axis=-1, keepdims=True))
        acc.append(_dot(p_new, vnew_ref[:, ln]))

    for c in range(n_chunks):
        i = n_chunks + c
        wait_load(i)
        ks = slice(c * ch_keys, (c + 1) * ch_keys)
        for g in range(N_KV_HEADS):
            acc[g] = acc[g] + _dot(sc_ref[g, :, ks], head_rows(i % 2, g))
        if i + 2 < n_loads:
            start_load(i + 2)

    heads = []
    for g in range(N_KV_HEADS):
        og = acc[g] / denom[g]
        for hh in range(N_HEADS // N_KV_HEADS):
            heads.append(og[hh * n_tok:(hh + 1) * n_tok])
    o = jnp.concatenate(heads, axis=-1)
    out_ref[...] = h_ref[...] + _dot(o, wout_ref[...])


def _dsa_sample(h, q, k, v, qi, kw, cache_k, cache_v, cache_kidx, page_table, layer, w_out):
    B, n_tok, D = h.shape
    grp = N_HEADS // N_KV_HEADS
    assert grp * n_tok == Q_ROWS
    n_pages = page_table.shape[1]
    assert n_pages % PAGES_PER_CHUNK == 0
    past = n_pages * PAGE_SIZE
    topk = min(TOPK_MAX, (past + n_tok) // 4)
    q8 = q.reshape(B, n_tok, N_KV_HEADS, grp, HEAD_DIM).transpose(0, 2, 3, 1, 4).reshape(B, N_KV_HEADS, Q_ROWS, HEAD_DIM)
    rep = lambda a: jnp.concatenate([a] * grp, axis=1)
    qi_rows = rep(qi.reshape(B, n_tok, IDX_HEADS, IDX_DIM)).reshape(B, Q_ROWS * IDX_HEADS, IDX_DIM)
    wq_rows = rep(kw[:, :, IDX_DIM:IDX_DIM + IDX_HEADS]).reshape(B, Q_ROWS * IDX_HEADS, 1)
    pad8 = lambda a: jnp.pad(a, ((0, 0), (0, Q_ROWS - n_tok), (0, 0)))
    ck = cache_k.reshape(cache_k.shape[0], cache_k.shape[1], PAGE_SIZE * N_KV_HEADS, HEAD_DIM)
    cv = cache_v.reshape(ck.shape)
    bspec = lambda *s: pl.BlockSpec((None,) + s, lambda b, pt: (b,) + (0,) * len(s))
    cspec = lambda s: pl.BlockSpec(s, lambda b, pt: (0,) * len(s), pipeline_mode=pl.Buffered(1))
    any_spec = pl.BlockSpec(memory_space=pl.ANY)
    grid_spec = pltpu.PrefetchScalarGridSpec(
        num_scalar_prefetch=1,
        grid=(B,),
        in_specs=[bspec(N_KV_HEADS, Q_ROWS, HEAD_DIM), bspec(Q_ROWS * IDX_HEADS, IDX_DIM), bspec(Q_ROWS * IDX_HEADS, 1),
                  bspec(Q_ROWS, k.shape[-1]), bspec(Q_ROWS, v.shape[-1]), bspec(Q_ROWS, LANES), bspec(n_tok, D),
                  cspec(w_out.shape), cspec((LANES, LANES)), any_spec, any_spec, any_spec],
        out_specs=bspec(n_tok, D),
        scratch_shapes=[pltpu.VMEM((n_pages, PAGE_SIZE, IDX_DIM), F32),
                        pltpu.VMEM((2, PAGES_PER_CHUNK * PAGE_SIZE * N_KV_HEADS, HEAD_DIM), F32),
                        pltpu.VMEM((N_KV_HEADS, Q_ROWS, past), F32),
                        pltpu.SemaphoreType.DMA((3,))],
    )
    return pl.pallas_call(
        functools.partial(_dsa_sample_body, layer, topk, n_pages),
        grid_spec=grid_spec,
        out_shape=jax.ShapeDtypeStruct(h.shape, F32),
        compiler_params=pltpu.CompilerParams(dimension_semantics=("arbitrary",),
                                             vmem_limit_bytes=VMEM_LIMIT),
        name="dsa_sample",
    )(page_table, q8, qi_rows, wq_rows, pad8(k), pad8(v), pad8(kw), h, w_out.astype(BF16), _tri(LANES),
      cache_kidx, ck, cv)


PROMPT_T_TILE = 64
PROJ_ROW_TILE = 512
PROMPT_Q_TILE = 128


def kernel(x_prompt, x_sample, cache_k, cache_v, cache_kidx, state_rec_conv, state_rec_h, state_pool, state_ffn_conv, page_table, p_prompt, p_sample, norm_mix, norm_ffn, norm_ple, w_in_rec, conv_rec_w, conv_rec_b, w_rgate, b_rgate, w_igate, b_igate, lru_lambda, w_pool, pool_scale, w_out_rec, w_in_attn, q_norm, k_norm, w_out_attn, w_up, conv_ff_w, conv_ff_b, w_down, w_ple, w_ple_gate):
    B, T, D = x_prompt.shape
    Bs, Ts, _ = x_sample.shape
    depth = norm_mix.shape[0]
    C = conv_rec_w.shape[-1]
    FF = conv_ff_w.shape[-1]
    past = page_table.shape[1] * PAGE_SIZE
    hp = x_prompt
    hs = jnp.swapaxes(x_sample, 0, 1)
    zeros = lambda *s: jnp.zeros(s, F32)
    outs = {n: [] for n in ("rc_p", "rc_s", "rh_p", "rh_s", "pl_p", "pl_s", "k_p", "k_s", "v_p", "v_s",
                            "ki_p", "ki_s", "fc_p", "fc_s")}
    for i in range(depth):
        j = i // 2
        if i % 2 == 0:
            wts = (norm_mix[i], w_in_rec[j], conv_rec_w[j], conv_rec_b[j], w_rgate[j], b_rgate[j],
                   w_igate[j], b_igate[j], lru_lambda[j], w_pool[j], pool_scale[j], w_out_rec[j])
            hp, c, hh, pb = _rec_layer(hp, 1, PROMPT_T_TILE, 0, zeros(B, HIST_CONV, C), zeros(B, 1, C),
                                       zeros(B, HIST_POOL, C), *wts)
            outs["rc_p"].append(_unpad_hist(c, CONV_REC - 1, 1))
            outs["rh_p"].append(hh[:, 0])
            outs["pl_p"].append(_unpad_hist(pb, POOL_BUF, 1))
            hs, c, hh, pb = _rec_layer(hs, 0, Ts, past, _pad_hist(state_rec_conv[j], HIST_CONV, 0),
                                       state_rec_h[j][None], _pad_hist(state_pool[j], HIST_POOL, 0), *wts)
            outs["rc_s"].append(_unpad_hist(c, CONV_REC - 1, 0))
            outs["rh_s"].append(hh[0])
            outs["pl_s"].append(_unpad_hist(pb, POOL_BUF, 0))
        else:
            pw = (norm_mix[i], w_in_attn[j], q_norm[j], k_norm[j])
            q, k, v, qi, kw = _attn_project(hp.reshape(B * T, D), jnp.arange(T), PROJ_ROW_TILE, *pw)
            r3 = lambda a: a.reshape(B, T, -1)
            hp = _dsa_prompt(hp, r3(q), r3(k), r3(v), r3(qi), r3(kw), w_out_attn[j], PROMPT_Q_TILE)
            outs["k_p"].append(k.reshape(B, T, N_KV_HEADS, HEAD_DIM))
            outs["v_p"].append(v.reshape(B, T, N_KV_HEADS, HEAD_DIM))
            outs["ki_p"].append(r3(kw)[..., :IDX_DIM])
            pos_s = past + jnp.arange(Ts * Bs) // Bs
            q, k, v, qi, kw = _attn_project(hs.reshape(Ts * Bs, D), pos_s, Ts * Bs, *pw)
            bm = lambda a: jnp.swapaxes(a.reshape(Ts, Bs, -1), 0, 1)
            k, v, kw = bm(k), bm(v), bm(kw)
            hs = jnp.swapaxes(_dsa_sample(bm(hs), bm(q), k, v, bm(qi), kw, cache_k, cache_v, cache_kidx,
                                          page_table, j, w_out_attn[j]), 0, 1)
            outs["k_s"].append(k.reshape(Bs, Ts, N_KV_HEADS, HEAD_DIM))
            outs["v_s"].append(v.reshape(Bs, Ts, N_KV_HEADS, HEAD_DIM))
            outs["ki_s"].append(kw[..., :IDX_DIM])
        fw = (norm_ffn[i], w_up[i], conv_ff_w[i], conv_ff_b[i], w_down[i], norm_ple[i], w_ple[i], w_ple_gate[i])
        hp, fb = _ffn_layer(hp, p_prompt[i], 1, PROMPT_T_TILE, zeros(B, HIST_FFN, FF), *fw)
        outs["fc_p"].append(_unpad_hist(fb, CONV_FF - 1, 1))
        hs, fb = _ffn_layer(hs, jnp.swapaxes(p_sample[i], 0, 1), 0, Ts, _pad_hist(state_ffn_conv[i], HIST_FFN, 0), *fw)
        outs["fc_s"].append(_unpad_hist(fb, CONV_FF - 1, 0))
    st = lambda n: jnp.stack(outs[n])
    return (hp, jnp.swapaxes(hs, 0, 1), st("rc_p"), st("rc_s"), st("rh_p"), st("rh_s"), st("pl_p"), st("pl_s"),
            st("k_p"), st("k_s"), st("v_p"), st("v_s"), st("ki_p"), st("ki_s"), st("fc_p"), st("fc_s"))
```

```python
import functools
import math

import jax
import jax.numpy as jnp
from jax import lax
from jax.experimental import pallas as pl
from jax.experimental.pallas import tpu as pltpu

F32 = jnp.float32
BF16 = jnp.bfloat16
EPS = 1e-6
LRU_C = 8.0
POOL_WINDOWS = (2, 4, 8, 16)
REC_HEADS = 8
CONV_REC = 4
CONV_FF = 3
POOL_BUF = max(POOL_WINDOWS) - 1
N_HEADS = 8
HEAD_DIM = 128
N_KV_HEADS = 4
IDX_HEADS = 8
IDX_DIM = 64
TOPK_MAX = 256
ROPE_THETA = 10000.0
PAGE_SIZE = 128

LANES = 128
SUBLANES = 8
VMEM_LIMIT = 56 * 1024 * 1024

HIST_CONV = SUBLANES
HIST_POOL = 2 * SUBLANES
HIST_FFN = SUBLANES


def _rms(x, g):
    return x * lax.rsqrt(jnp.mean(x * x, axis=-1, keepdims=True) + EPS) * g


def _dot(a, b):
    return jnp.dot(a.astype(BF16), b.astype(BF16), preferred_element_type=F32)


def _dot_nt(a, b):
    return lax.dot_general(a.astype(BF16), b.astype(BF16), (((1,), (1,)), ((), ())),
                           preferred_element_type=F32)


def _softplus(x):
    return jnp.maximum(x, 0.0) + jnp.log1p(jnp.exp(-jnp.abs(x)))


def _expm1(x):
    u = jnp.exp(x)
    near = (u - 1.0) * x / jnp.log(u)
    return jnp.where(u == 1.0, x, jnp.where(jnp.abs(x) > 0.5, u - 1.0, near))


def _tix(ta, start, size):
    t = slice(start, start + size) if isinstance(start, int) else pl.ds(start, size)
    return (slice(None), t, slice(None)) if ta == 1 else (t, slice(None), slice(None))


def _const_spec(shape):
    nd = len(shape)
    return pl.BlockSpec(shape, lambda *_: (0,) * nd, pipeline_mode=pl.Buffered(1))


def _tile_spec(shape, ta):
    if ta == 1:
        return pl.BlockSpec(shape, lambda i: (0, i, 0))
    return pl.BlockSpec(shape, lambda i: (i, 0, 0))


def _rec_body(ta, pos0, h_ref, g_ref, win_ref, cw_ref, cb_ref, wr_ref, br_ref, wi_ref, bi_ref,
              lam_ref, wp_ref, ps_ref, wout_ref, conv0_ref, h0_ref, pool0_ref,
              out_ref, conv_out_ref, hlast_ref, pool_out_ref,
              xa_ext, xb_ext, a_s, u_s, hs_s, hc_s):
    step = pl.program_id(0)
    A, S, D = h_ref.shape
    R = A * S
    n_t = S if ta == 1 else A
    C = xa_ext.shape[-1]

    @pl.when(step == 0)
    def _():
        xa_ext[_tix(ta, 0, HIST_CONV)] = conv0_ref[...]
        xb_ext[_tix(ta, 0, HIST_POOL)] = pool0_ref[...]
        hc_s[...] = h0_ref[...]

    x = h_ref[...].reshape(R, D)
    z = _dot(_rms(x, g_ref[...]), win_ref[...])
    ga = z[:, C:2 * C]
    xa_ext[_tix(ta, HIST_CONV, n_t)] = z[:, :C].reshape(A, S, C)
    xb_ext[_tix(ta, HIST_POOL, n_t)] = z[:, 2 * C:].reshape(A, S, C)

    xc = cb_ref[...][None]
    for k in range(CONV_REC):
        xc = xc + xa_ext[_tix(ta, HIST_CONV - (CONV_REC - 1) + k, n_t)] * cw_ref[k:k + 1, :][None]
    xc = xc.reshape(R, C)

    r = jax.nn.sigmoid(_dot(xc, wr_ref[...]) + br_ref[...])
    gi = jax.nn.sigmoid(_dot(xc, wi_ref[...]) + bi_ref[...])
    log_a = (-LRU_C * r) * _softplus(-lam_ref[...])
    a_s[...] = jnp.exp(log_a).reshape(A, S, C)
    u_s[...] = (jnp.sqrt(-_expm1(2.0 * log_a)) * gi * xc).reshape(A, S, C)

    def scan_step(t, h):
        ix = _tix(ta, t, 1)
        h = a_s[ix] * h + u_s[ix]
        hs_s[ix] = h
        return h

    if n_t <= SUBLANES:
        h = hc_s[...]
        for t in range(n_t):
            h = scan_step(t, h)
    else:
        h = lax.fori_loop(0, n_t, scan_step, hc_s[...], unroll=SUBLANES)
    hc_s[...] = h
    hlast_ref[...] = h
    ya = hs_s[...].reshape(R, C) * jax.nn.gelu(ga)

    G = C // len(POOL_WINDOWS)
    pos = pos0 + step * n_t + lax.broadcasted_iota(jnp.int32, (A, S, G), ta)
    posf = pos.astype(F32) + 1.0
    ds = []
    for gidx, w in enumerate(POOL_WINDOWS):
        ln = slice(gidx * G, (gidx + 1) * G)
        cur = xb_ext[_tix(ta, HIST_POOL, n_t)[:2] + (ln,)]
        acc = cur
        for j in range(1, w):
            acc = acc + xb_ext[_tix(ta, HIST_POOL - j, n_t)[:2] + (ln,)]
        ds.append(acc / jnp.minimum(jnp.float32(w), posf) - cur)
    d = jnp.concatenate(ds, axis=-1).reshape(R, C)
    yb = _dot(d, wp_ref[...]) * ps_ref[...]

    y = _dot(ya, wout_ref[:C, :]) + _dot(yb, wout_ref[C:, :])
    out_ref[...] = (x + y).reshape(A, S, D)

    new_conv = xa_ext[_tix(ta, n_t, HIST_CONV)]
    xa_ext[_tix(ta, 0, HIST_CONV)] = new_conv
    conv_out_ref[...] = new_conv
    new_pool = xb_ext[_tix(ta, n_t, HIST_POOL)]
    xb_ext[_tix(ta, 0, HIST_POOL)] = new_pool
    pool_out_ref[...] = new_pool


def _block_diag(w):
    H, a, b = w.shape
    eye = jnp.eye(H, dtype=w.dtype)
    return (eye[:, None, :, None] * w[:, :, None, :]).reshape(H * a, H * b)


def _rec_layer(h, ta, t_tile, pos0, conv0, h0, pool0, norm_g, w_in, conv_w, conv_b,
               w_r, b_r, w_i, b_i, lam, w_pool, pool_scale, w_out):
    D = h.shape[-1]
    C = conv_w.shape[-1]
    n_time = h.shape[ta]
    n_batch = h.shape[1 - ta]
    assert n_time % t_tile == 0
    if ta == 1:
        blk = (n_batch, t_tile, D)
        ext = lambda hist: (n_batch, hist + t_tile, C)
        hist_shape = lambda hist: (n_batch, hist, C)
        row_shape = (n_batch, 1, C)
        tile_c = (n_batch, t_tile, C)
    else:
        blk = (t_tile, n_batch, D)
        ext = lambda hist: (hist + t_tile, n_batch, C)
        hist_shape = lambda hist: (hist, n_batch, C)
        row_shape = (1, n_batch, C)
        tile_c = (t_tile, n_batch, C)
    row = lambda v: v.reshape(1, -1)
    args = (h, row(norm_g), w_in.astype(BF16), conv_w, row(conv_b),
            _block_diag(w_r).astype(BF16), row(b_r), _block_diag(w_i).astype(BF16), row(b_i),
            row(lam), _block_diag(w_pool).astype(BF16), row(pool_scale), w_out.astype(BF16),
            conv0, h0, pool0)
    in_specs = [_tile_spec(blk, ta)] + [_const_spec(a.shape) for a in args[1:]]
    out_shape = (jax.ShapeDtypeStruct(h.shape, F32),
                 jax.ShapeDtypeStruct(hist_shape(HIST_CONV), F32),
                 jax.ShapeDtypeStruct(row_shape, F32),
                 jax.ShapeDtypeStruct(hist_shape(HIST_POOL), F32))
    out_specs = (_tile_spec(blk, ta),
                 pl.BlockSpec(hist_shape(HIST_CONV), lambda i: (0, 0, 0)),
                 pl.BlockSpec(row_shape, lambda i: (0, 0, 0)),
                 pl.BlockSpec(hist_shape(HIST_POOL), lambda i: (0, 0, 0)))
    scratch = [pltpu.VMEM(ext(HIST_CONV), F32), pltpu.VMEM(ext(HIST_POOL), F32),
               pltpu.VMEM(tile_c, F32), pltpu.VMEM(tile_c, F32), pltpu.VMEM(tile_c, F32),
               pltpu.VMEM(row_shape, F32)]
    return pl.pallas_call(
        functools.partial(_rec_body, ta, pos0),
        grid=(n_time // t_tile,),
        in_specs=in_specs, out_specs=out_specs, out_shape=out_shape, scratch_shapes=scratch,
        compiler_params=pltpu.CompilerParams(dimension_semantics=("arbitrary",),
                                             vmem_limit_bytes=VMEM_LIMIT),
        name="rec_pool_mixer",
    )(*args)


def _ffn_body(ta, n_chunks, h_ref, p_ref, gf_ref, wup_ref, cw_ref, cb_ref, wdn_ref, gp_ref,
              wple_ref, wpg_ref, buf0_ref, out_ref, buf_out_ref, g_ext):
    step = pl.program_id(0)
    A, S, D = h_ref.shape
    R = A * S
    n_t = S if ta == 1 else A
    FF = g_ext.shape[-1]
    ch = FF // n_chunks

    @pl.when(step == 0)
    def _():
        g_ext[_tix(ta, 0, HIST_FFN)] = buf0_ref[...]

    x = h_ref[...].reshape(R, D)
    xn = _rms(x, gf_ref[...]).astype(BF16)
    acc = jnp.zeros((R, D), F32)
    for c in range(n_chunks):
        ln = slice(c * ch, (c + 1) * ch)
        g = jnp.dot(xn, wup_ref[:, ln], preferred_element_type=F32)
        u = jnp.dot(xn, wup_ref[:, FF + c * ch:FF + (c + 1) * ch], preferred_element_type=F32)
        g_ext[_tix(ta, HIST_FFN, n_t)[:2] + (ln,)] = g.reshape(A, S, ch)
        gc = cb_ref[:, ln][None]
        for k in range(CONV_FF):
            gc = gc + (g_ext[_tix(ta, HIST_FFN - (CONV_FF - 1) + k, n_t)[:2] + (ln,)]
                       * cw_ref[k:k + 1, ln][None])
        act = jax.nn.gelu(gc).reshape(R, ch) * u
        acc = acc + _dot(act, wdn_ref[ln, :])
    h1 = x + acc
    gate = jax.nn.sigmoid(_dot(_rms(h1, gp_ref[...]), wpg_ref[...]))
    pp = _dot(p_ref[...].reshape(R, p_ref.shape[-1]), wple_ref[...])
    out_ref[...] = (h1 + pp * gate).reshape(A, S, D)

    new_buf = g_ext[_tix(ta, n_t, HIST_FFN)]
    g_ext[_tix(ta, 0, HIST_FFN)] = new_buf
    buf_out_ref[...] = new_buf


def _ffn_layer(h, p, ta, t_tile, buf0, norm_f, w_up, conv_w, conv_b, w_down, norm_p, w_ple, w_pg,
               n_chunks=2):
    D = h.shape[-1]
    FF = conv_w.shape[-1]
    n_time = h.shape[ta]
    n_batch = h.shape[1 - ta]
    assert n_time % t_tile == 0 and FF % (n_chunks * LANES) == 0
    if ta == 1:
        blk = lambda c: (n_batch, t_tile, c)
        ext = (n_batch, HIST_FFN + t_tile, FF)
        hist = (n_batch, HIST_FFN, FF)
    else:
        blk = lambda c: (t_tile, n_batch, c)
        ext = (HIST_FFN + t_tile, n_batch, FF)
        hist = (HIST_FFN, n_batch, FF)
    row = lambda v: v.reshape(1, -1)
    args = (h, p, row(norm_f), w_up.astype(BF16), conv_w, row(conv_b), w_down.astype(BF16),
            row(norm_p), w_ple.astype(BF16), w_pg.astype(BF16), buf0)
    in_specs = ([_tile_spec(blk(D), ta), _tile_spec(blk(p.shape[-1]), ta)]
                + [_const_spec(a.shape) for a in args[2:]])
    out_shape = (jax.ShapeDtypeStruct(h.shape, F32), jax.ShapeDtypeStruct(hist, F32))
    out_specs = (_tile_spec(blk(D), ta), pl.BlockSpec(hist, lambda i: (0, 0, 0)))
    return pl.pallas_call(
        functools.partial(_ffn_body, ta, n_chunks),
        grid=(n_time // t_tile,),
        in_specs=in_specs, out_specs=out_specs, out_shape=out_shape,
        scratch_shapes=[pltpu.VMEM(ext, F32)],
        compiler_params=pltpu.CompilerParams(dimension_semantics=("arbitrary",),
                                             vmem_limit_bytes=VMEM_LIMIT),
        name="conv_ffn_ple",
    )(*args)


def _pad_hist(state, hist, ta):
    n = state.shape[1]
    if ta == 1:
        return jnp.pad(state, ((0, 0), (hist - n, 0), (0, 0)))
    return jnp.pad(jnp.swapaxes(state, 0, 1), ((hist - n, 0), (0, 0), (0, 0)))


def _unpad_hist(block, n, ta):
    if ta == 1:
        return block[:, block.shape[1] - n:, :]
    return jnp.swapaxes(block[block.shape[0] - n:], 0, 1)


Q_OFF = 0
K_OFF = N_HEADS * HEAD_DIM
V_OFF = K_OFF + N_KV_HEADS * HEAD_DIM
QI_OFF = V_OFF + N_KV_HEADS * HEAD_DIM
KW_OFF = QI_OFF + IDX_HEADS * IDX_DIM
ATTN_COLS = KW_OFF + LANES
IDX_SCALE = (IDX_DIM ** -0.5) * (IDX_HEADS ** -0.5)
ATTN_SCALE = HEAD_DIM ** -0.5


def _rope_tables(pos, dim):
    half = dim // 2
    inv = jnp.power(ROPE_THETA, -jnp.arange(half, dtype=F32) / half)
    ang = pos.astype(F32)[:, None] * inv[None, :]
    cos, sin = jnp.cos(ang), jnp.sin(ang)
    reps = LANES // dim
    return (jnp.tile(jnp.concatenate([cos, cos], axis=-1), (1, reps)),
            jnp.tile(jnp.concatenate([-sin, sin], axis=-1), (1, reps)))


def _proj_body(x_ref, g_ref, w_ref, qn_ref, kn_ref, cq_ref, sq_ref, ci_ref, si_ref,
               q_ref, k_ref, v_ref, kb_ref, vb_ref, qi_ref, kw_ref):
    z = _dot(_rms(x_ref[...], g_ref[...]), w_ref[...])
    cq, sq, ci, si = cq_ref[...], sq_ref[...], ci_ref[...], si_ref[...]
    lane = lax.broadcasted_iota(jnp.int32, cq.shape, 1)
    low_half = (lane % IDX_DIM) < (IDX_DIM // 2)

    def rope_head(t):
        return t * cq + pltpu.roll(t, HEAD_DIM // 2, 1) * sq

    def rope_idx(t):
        partner = jnp.where(low_half, pltpu.roll(t, LANES - IDX_DIM // 2, 1), pltpu.roll(t, IDX_DIM // 2, 1))
        return t * ci + partner * si

    def head_norm(t, g):
        return t * lax.rsqrt(jnp.mean(t * t, axis=-1, keepdims=True) + EPS) * g

    for h in range(N_HEADS):
        ln = slice(h * HEAD_DIM, (h + 1) * HEAD_DIM)
        q_ref[:, ln] = (rope_head(head_norm(z[:, ln], qn_ref[...])) * ATTN_SCALE).astype(BF16)
    for h in range(N_KV_HEADS):
        ln = slice(h * HEAD_DIM, (h + 1) * HEAD_DIM)
        kh = rope_head(head_norm(z[:, K_OFF + h * HEAD_DIM:K_OFF + (h + 1) * HEAD_DIM], kn_ref[...]))
        k_ref[:, ln] = kh
        kb_ref[:, ln] = kh.astype(BF16)
    v = z[:, V_OFF:QI_OFF]
    v_ref[...] = v
    vb_ref[...] = v.astype(BF16)
    for s in range(IDX_HEADS * IDX_DIM // LANES):
        ln = slice(s * LANES, (s + 1) * LANES)
        qi_ref[:, ln] = rope_idx(z[:, QI_OFF + s * LANES:QI_OFF + (s + 1) * LANES]).astype(BF16)
    kw = z[:, KW_OFF:]
    kw_ref[...] = jnp.where(lane < IDX_DIM, rope_idx(kw), kw * IDX_SCALE)


def _attn_project(x, pos_tab, r_tile, norm_g, w_in, q_norm, k_norm):
    N, D = x.shape
    P = pos_tab.shape[0]
    assert N % r_tile == 0 and P % r_tile == 0
    w = jnp.pad(w_in, ((0, 0), (0, ATTN_COLS - w_in.shape[1]))).astype(BF16)
    cq, sq = _rope_tables(pos_tab, HEAD_DIM)
    ci, si = _rope_tables(pos_tab, IDX_DIM)
    n_tab = P // r_tile
    row_spec = lambda c: pl.BlockSpec((r_tile, c), lambda i: (i, 0))
    tab_spec = pl.BlockSpec((r_tile, LANES), lambda i: (i % n_tab, 0))
    row = lambda v: v.reshape(1, -1)
    kv_w = N_KV_HEADS * HEAD_DIM
    outs = ((N_HEADS * HEAD_DIM, BF16), (kv_w, F32), (kv_w, F32), (kv_w, BF16), (kv_w, BF16),
            (IDX_HEADS * IDX_DIM, BF16), (LANES, F32))
    return pl.pallas_call(
        _proj_body,
        grid=(N // r_tile,),
        in_specs=[row_spec(D), _const_spec((1, D)), _const_spec(w.shape), _const_spec((1, HEAD_DIM)),
                  _const_spec((1, HEAD_DIM)), tab_spec, tab_spec, tab_spec, tab_spec],
        out_specs=tuple(row_spec(c) for c, _ in outs),
        out_shape=tuple(jax.ShapeDtypeStruct((N, c), dt) for c, dt in outs),
        compiler_params=pltpu.CompilerParams(dimension_semantics=("arbitrary",),
                                             vmem_limit_bytes=VMEM_LIMIT),
        name="attn_project",
    )(x, row(norm_g), w, row(q_norm), row(k_norm), cq, sq, ci, si)


KEY_POS_INF = 0x7F800000
KEY_NEG_INF = -0x7F800001
REFINE_STEPS = 8


def _key_to_float(key):
    bits = key ^ (jnp.right_shift(key, 31) & jnp.int32(0x7FFFFFFF))
    return lax.bitcast_convert_type(bits, F32)


def _topk_bias(segs, k, tri_ref):
    kf = jnp.float32(k)

    def count_ge(t):
        tot = None
        for x, _ in segs:
            c = jnp.sum(jnp.where(x >= t, 1.0, 0.0), axis=-1, keepdims=True)
            tot = c if tot is None else tot + c
        return tot

    def enough(cand):
        ok = (cand < KEY_NEG_INF) | (count_ge(_key_to_float(cand)) >= kf)
        return ok & (cand <= KEY_POS_INF)

    zero = jnp.zeros((segs[0][0].shape[0], 1), jnp.int32)
    r = jnp.where(enough(zero), zero, jnp.int32(-2 ** 31))

    def bit_step(i, r):
        cand = r | jnp.left_shift(jnp.int32(1), 30 - i)
        return jnp.where(enough(cand), cand, r)

    r = lax.fori_loop(0, 31, bit_step, r)
    lo = _key_to_float(r)
    hi = jnp.where(r >= KEY_POS_INF, jnp.inf, _key_to_float(r + 1))

    def refine(_, lh):
        lo, hi = lh
        mid = lo * 0.5 + hi * 0.5
        up = count_ge(mid) >= kf
        return jnp.where(up, mid, lo), jnp.where(up, hi, mid)

    lo, hi = lax.fori_loop(0, REFINE_STEPS, refine, (lo, hi))
    need = kf - count_ge(hi)
    off = jnp.zeros_like(need)
    out = []
    for x, valid_fn in segs:
        n = x.shape[-1]
        w = min(n, LANES)
        pieces = []
        for c in range(n // w):
            xc = x[:, c * w:(c + 1) * w]
            above = jnp.where(xc >= hi, 1.0, 0.0)
            tie = jnp.where(xc >= lo, 1.0, 0.0) - above
            pref = jnp.dot(tie.astype(BF16), tri_ref[:w, :w], preferred_element_type=F32)
            take = above + tie * jnp.where(off + pref <= need, 1.0, 0.0)
            pieces.append(jnp.where(valid_fn(c * w, w), jnp.where(take > 0.5, 0.0, -jnp.inf), -jnp.inf))
            off = off + pref[:, w - 1:w]
        out.append(pieces[0] if len(pieces) == 1 else jnp.concatenate(pieces, axis=-1))
    return out


def _dsa_prompt_body(topk, tile0, q_ref, qi_ref, kwq_ref, k_ref, v_ref, kwk_ref, h_ref, wout_ref, tri_ref, out_ref):
    Tq = q_ref.shape[0]
    S = k_ref.shape[0]
    grp = N_HEADS // N_KV_HEADS
    qpos0 = (tile0 + pl.program_id(1)) * Tq

    def causal(c0, w):
        qpos = qpos0 + lax.broadcasted_iota(jnp.int32, (Tq, w), 0)
        return c0 + lax.broadcasted_iota(jnp.int32, (Tq, w), 1) <= qpos

    ki = kwk_ref[:, :IDX_DIM].astype(BF16)
    wq = kwq_ref[...]
    scores = None
    for h in range(IDX_HEADS):
        s = _dot_nt(qi_ref[:, h * IDX_DIM:(h + 1) * IDX_DIM], ki)
        t = jnp.maximum(s, 0.0) * wq[:, IDX_DIM + h:IDX_DIM + h + 1]
        scores = t if scores is None else scores + t
    scores = jnp.where(causal(0, S), scores, -jnp.inf)
    (bias,) = _topk_bias([(scores, causal)], topk, tri_ref)

    outs = []
    for g in range(N_KV_HEADS):
        ln = slice(g * HEAD_DIM, (g + 1) * HEAD_DIM)
        qs = jnp.concatenate([q_ref[:, (g * grp + hh) * HEAD_DIM:(g * grp + hh + 1) * HEAD_DIM] for hh in range(grp)],
                             axis=0)
        s = _dot_nt(qs, k_ref[:, ln])
        ps, ls = [], []
        for hh in range(grp):
            sh = s[hh * Tq:(hh + 1) * Tq] + bias
            p = jnp.exp(sh - jnp.max(sh, axis=-1, keepdims=True))
            ls.append(jnp.sum(p, axis=-1, keepdims=True))
            ps.append(p.astype(BF16))
        o = _dot(jnp.concatenate(ps, axis=0), v_ref[:, ln])
        outs += [o[hh * Tq:(hh + 1) * Tq] / ls[hh] for hh in range(grp)]
    out_ref[...] = h_ref[...] + _dot(jnp.concatenate(outs, axis=-1), wout_ref[...])


def _tri(n):
    i = jnp.arange(n)
    return (i[:, None] <= i[None, :]).astype(BF16)


def _dsa_prompt(h, q, k, v, qi, kw, w_out, q_tile):
    B, T, D = h.shape
    topk = min(TOPK_MAX, T // 4)
    wb, tri = w_out.astype(BF16), _tri(LANES)
    for tile in range(T // q_tile):
        S = (tile + 1) * q_tile
        qspec = lambda c, tile=tile: pl.BlockSpec((None, q_tile, c), lambda b, j: (b, tile + j, 0))
        kspec = lambda c, S=S: pl.BlockSpec((None, S, c), lambda b, j: (b, 0, 0))
        h = pl.pallas_call(
            functools.partial(_dsa_prompt_body, topk, tile),
            grid=(B, 1),
            in_specs=[qspec(q.shape[-1]), qspec(qi.shape[-1]), qspec(LANES), kspec(k.shape[-1]), kspec(v.shape[-1]),
                      kspec(LANES), qspec(D), _const_spec(wb.shape), _const_spec((LANES, LANES))],
            out_specs=qspec(D),
            out_shape=jax.ShapeDtypeStruct(h.shape, F32),
            input_output_aliases={6: 0},
            compiler_params=pltpu.CompilerParams(dimension_semantics=("arbitrary", "arbitrary"),
                                                 vmem_limit_bytes=VMEM_LIMIT),
            name="dsa_prompt",
        )(q, qi, kw, k, v, kw, h, wb, tri)
    return h


PAGES_PER_CHUNK = 16
Q_ROWS = SUBLANES


def _dsa_sample_body(layer, topk, n_pages, pt_ref,
                     q8_ref, qi_ref, wq_ref, knew_ref, vnew_ref, kwnew_ref, h_ref, wout_ref, tri_ref,
                     ckidx_ref, ck_ref, cv_ref, out_ref, kidx_buf, kv_buf, sc_ref, sems):
    b = pl.program_id(0)
    nb = pl.num_programs(0)
    n_chunks = n_pages // PAGES_PER_CHUNK
    n_loads = 2 * n_chunks
    page_rows = PAGE_SIZE * N_KV_HEADS
    ch_keys = PAGES_PER_CHUNK * PAGE_SIZE
    n_tok = h_ref.shape[0]

    def kidx_copy(bb, p):
        return pltpu.make_async_copy(ckidx_ref.at[layer, pt_ref[bb, p]], kidx_buf.at[bb % 2, p], sems.at[bb % 2])

    def kv_copy(bb, i, p):
        src = ck_ref if i < n_chunks else cv_ref
        page = pt_ref[bb, (i % n_chunks) * PAGES_PER_CHUNK + p]
        return pltpu.make_async_copy(src.at[layer, page], kv_buf.at[i % 2, pl.ds(p * page_rows, page_rows)],
                                     sems.at[2 + i % 2])

    def start_kidx(bb):
        for p in range(n_pages):
            kidx_copy(bb, p).start()

    def start_load(bb, i):
        for p in range(PAGES_PER_CHUNK):
            kv_copy(bb, i, p).start()

    def wait_load(i):
        for p in range(PAGES_PER_CHUNK):
            kv_copy(b, i, p).wait()

    def start_after(i):
        if i + 2 < n_loads:
            start_load(b, i + 2)
        else:
            @pl.when(b + 1 < nb)
            def _():
                start_load(b + 1, i + 2 - n_loads)

    def head_rows(slot, g):
        return kv_buf[slot, pl.ds(g, ch_keys, stride=N_KV_HEADS), :]

    @pl.when(b == 0)
    def _():
        start_kidx(b)
        start_load(b, 0)
        start_load(b, 1)

    for p in range(n_pages):
        kidx_copy(b, p).wait()

    @pl.when(b + 1 < nb)
    def _():
        start_kidx(b + 1)

    qi = qi_ref[...]
    wq = wq_ref[...]

    def head_sum(s):
        n = s.shape[-1]
        return jnp.sum((jnp.maximum(s, 0.0) * wq).reshape(Q_ROWS, IDX_HEADS, n), axis=1)

    i_past = jnp.concatenate([head_sum(_dot(qi, kidx_buf[b % 2, p])) for p in range(n_pages)], axis=-1)
    i_new = head_sum(_dot_nt(qi, kwnew_ref[:, :IDX_DIM]))
    tq = lax.broadcasted_iota(jnp.int32, i_new.shape, 0) % n_tok

    def valid_new(c0, w):
        return c0 + lax.broadcasted_iota(jnp.int32, (Q_ROWS, w), 1) <= tq[:, :w]

    def valid_past(c0, w):
        return lax.broadcasted_iota(jnp.int32, (Q_ROWS, w), 1) >= 0

    i_new = jnp.where(valid_new(0, i_new.shape[-1]), i_new, -jnp.inf)
    bias_past, bias_new = _topk_bias([(i_past, valid_past), (i_new, valid_new)], topk, tri_ref)

    for c in range(n_chunks):
        wait_load(c)
        ks = slice(c * ch_keys, (c + 1) * ch_keys)
        for g in range(N_KV_HEADS):
            sc_ref[g, :, ks] = _dot_nt(q8_ref[g], head_rows(c % 2, g)) + bias_past[:, ks]
        start_after(c)

    acc, denom = [], []
    for g in range(N_KV_HEADS):
        ln = slice(g * HEAD_DIM, (g + 1) * HEAD_DIM)
        s_new = _dot_nt(q8_ref[g], knew_ref[:, ln]) + bias_new
        s_past = sc_ref[g]
        m = jnp.maximum(jnp.max(s_past, axis=-1, keepdims=True), jnp.max(s_new, axis=-1, keepdims=True))
        p_past = jnp.exp(s_past - m)
        p_new = jnp.exp(s_new - m)
        sc_ref[g] = p_past
        denom.append(jnp.sum(p_past, axis=-1, keepdims=True) + jnp.sum(p_new, axis=-1, keepdims=True))
        acc.append(_dot(p_new, vnew_ref[:, ln]))

    for c in range(n_chunks):
        i = n_chunks + c
        wait_load(i)
        ks = slice(c * ch_keys, (c + 1) * ch_keys)
        for g in range(N_KV_HEADS):
            acc[g] = acc[g] + _dot(sc_ref[g, :, ks], head_rows(i % 2, g))
        start_after(i)

    heads = []
    for g in range(N_KV_HEADS):
        og = acc[g] / denom[g]
        for hh in range(N_HEADS // N_KV_HEADS):
            heads.append(og[hh * n_tok:(hh + 1) * n_tok])
    o = jnp.concatenate(heads, axis=-1)
    out_ref[...] = h_ref[...] + _dot(o, wout_ref[...])


def _dsa_sample(h, q, k, v, qi, kw, cache_k, cache_v, cache_kidx, page_table, layer, w_out):
    B, n_tok, D = h.shape
    grp = N_HEADS // N_KV_HEADS
    assert grp * n_tok == Q_ROWS
    n_pages = page_table.shape[1]
    assert n_pages % PAGES_PER_CHUNK == 0
    past = n_pages * PAGE_SIZE
    topk = min(TOPK_MAX, (past + n_tok) // 4)
    q8 = q.reshape(B, n_tok, N_KV_HEADS, grp, HEAD_DIM).transpose(0, 2, 3, 1, 4).reshape(B, N_KV_HEADS, Q_ROWS, HEAD_DIM)
    rep = lambda a: jnp.concatenate([a] * grp, axis=1)
    qi_rows = rep(qi.reshape(B, n_tok, IDX_HEADS, IDX_DIM)).reshape(B, Q_ROWS * IDX_HEADS, IDX_DIM)
    wq_rows = rep(kw[:, :, IDX_DIM:IDX_DIM + IDX_HEADS]).reshape(B, Q_ROWS * IDX_HEADS, 1)
    pad8 = lambda a: jnp.pad(a, ((0, 0), (0, Q_ROWS - n_tok), (0, 0)))
    ck = cache_k.reshape(cache_k.shape[0], cache_k.shape[1], PAGE_SIZE * N_KV_HEADS, HEAD_DIM)
    cv = cache_v.reshape(ck.shape)
    cki = jnp.swapaxes(cache_kidx, 2, 3)
    bspec = lambda *s: pl.BlockSpec((None,) + s, lambda b, pt: (b,) + (0,) * len(s))
    cspec = lambda s: pl.BlockSpec(s, lambda b, pt: (0,) * len(s), pipeline_mode=pl.Buffered(1))
    any_spec = pl.BlockSpec(memory_space=pl.ANY)
    grid_spec = pltpu.PrefetchScalarGridSpec(
        num_scalar_prefetch=1,
        grid=(B,),
        in_specs=[bspec(N_KV_HEADS, Q_ROWS, HEAD_DIM), bspec(Q_ROWS * IDX_HEADS, IDX_DIM), bspec(Q_ROWS * IDX_HEADS, 1),
                  bspec(Q_ROWS, k.shape[-1]), bspec(Q_ROWS, v.shape[-1]), bspec(Q_ROWS, LANES), bspec(n_tok, D),
                  cspec(w_out.shape), cspec((LANES, LANES)), any_spec, any_spec, any_spec],
        out_specs=bspec(n_tok, D),
        scratch_shapes=[pltpu.VMEM((2, n_pages, IDX_DIM, PAGE_SIZE), F32),
                        pltpu.VMEM((2, PAGES_PER_CHUNK * PAGE_SIZE * N_KV_HEADS, HEAD_DIM), F32),
                        pltpu.VMEM((N_KV_HEADS, Q_ROWS, past), F32),
                        pltpu.SemaphoreType.DMA((4,))],
    )
    return pl.pallas_call(
        functools.partial(_dsa_sample_body, layer, topk, n_pages),
        grid_spec=grid_spec,
        out_shape=jax.ShapeDtypeStruct(h.shape, F32),
        compiler_params=pltpu.CompilerParams(dimension_semantics=("arbitrary",),
                                             vmem_limit_bytes=VMEM_LIMIT),
        name="dsa_sample",
    )(page_table, q8, qi_rows, wq_rows, pad8(k), pad8(v), pad8(kw), h, w_out.astype(BF16), _tri(LANES),
      cki, ck, cv)


PROMPT_T_TILE = 64
PROJ_ROW_TILE = 512
PROMPT_Q_TILE = 256


def kernel(x_prompt, x_sample, cache_k, cache_v, cache_kidx, state_rec_conv, state_rec_h, state_pool, state_ffn_conv, page_table, p_prompt, p_sample, norm_mix, norm_ffn, norm_ple, w_in_rec, conv_rec_w, conv_rec_b, w_rgate, b_rgate, w_igate, b_igate, lru_lambda, w_pool, pool_scale, w_out_rec, w_in_attn, q_norm, k_norm, w_out_attn, w_up, conv_ff_w, conv_ff_b, w_down, w_ple, w_ple_gate):
    B, T, D = x_prompt.shape
    Bs, Ts, _ = x_sample.shape
    depth = norm_mix.shape[0]
    C = conv_rec_w.shape[-1]
    FF = conv_ff_w.shape[-1]
    past = page_table.shape[1] * PAGE_SIZE
    hp = x_prompt
    hs = jnp.swapaxes(x_sample, 0, 1)
    zeros = lambda *s: jnp.zeros(s, F32)
    outs = {n: [] for n in ("rc_p", "rc_s", "rh_p", "rh_s", "pl_p", "pl_s", "k_p", "k_s", "v_p", "v_s",
                            "ki_p", "ki_s", "fc_p", "fc_s")}
    for i in range(depth):
        j = i // 2
        if i % 2 == 0:
            wts = (norm_mix[i], w_in_rec[j], conv_rec_w[j], conv_rec_b[j], w_rgate[j], b_rgate[j],
                   w_igate[j], b_igate[j], lru_lambda[j], w_pool[j], pool_scale[j], w_out_rec[j])
            hp, c, hh, pb = _rec_layer(hp, 1, PROMPT_T_TILE, 0, zeros(B, HIST_CONV, C), zeros(B, 1, C),
                                       zeros(B, HIST_POOL, C), *wts)
            outs["rc_p"].append(_unpad_hist(c, CONV_REC - 1, 1))
            outs["rh_p"].append(hh[:, 0])
            outs["pl_p"].append(_unpad_hist(pb, POOL_BUF, 1))
            hs, c, hh, pb = _rec_layer(hs, 0, Ts, past, _pad_hist(state_rec_conv[j], HIST_CONV, 0),
                                       state_rec_h[j][None], _pad_hist(state_pool[j], HIST_POOL, 0), *wts)
            outs["rc_s"].append(_unpad_hist(c, CONV_REC - 1, 0))
            outs["rh_s"].append(hh[0])
            outs["pl_s"].append(_unpad_hist(pb, POOL_BUF, 0))
        else:
            pw = (norm_mix[i], w_in_attn[j], q_norm[j], k_norm[j])
            q, k, v, kb, vb, qi, kw = _attn_project(hp.reshape(B * T, D), jnp.arange(T), PROJ_ROW_TILE, *pw)
            r3 = lambda a: a.reshape(B, T, -1)
            hp = _dsa_prompt(hp, r3(q), r3(kb), r3(vb), r3(qi), r3(kw), w_out_attn[j], PROMPT_Q_TILE)
            outs["k_p"].append(k.reshape(B, T, N_KV_HEADS, HEAD_DIM))
            outs["v_p"].append(v.reshape(B, T, N_KV_HEADS, HEAD_DIM))
            outs["ki_p"].append(r3(kw)[..., :IDX_DIM])
            pos_s = past + jnp.arange(Ts * Bs) // Bs
            q, k, v, kb, vb, qi, kw = _attn_project(hs.reshape(Ts * Bs, D), pos_s, Ts * Bs, *pw)
            bm = lambda a: jnp.swapaxes(a.reshape(Ts, Bs, -1), 0, 1)
            k, v, kw = bm(k), bm(v), bm(kw)
            hs = jnp.swapaxes(_dsa_sample(bm(hs), bm(q), bm(kb), bm(vb), bm(qi), kw, cache_k, cache_v, cache_kidx,
                                          page_table, j, w_out_attn[j]), 0, 1)
            outs["k_s"].append(k.reshape(Bs, Ts, N_KV_HEADS, HEAD_DIM))
            outs["v_s"].append(v.reshape(Bs, Ts, N_KV_HEADS, HEAD_DIM))
            outs["ki_s"].append(kw[..., :IDX_DIM])
        fw = (norm_ffn[i], w_up[i], conv_ff_w[i], conv_ff_b[i], w_down[i], norm_ple[i], w_ple[i], w_ple_gate[i])
        hp, fb = _ffn_layer(hp, p_prompt[i], 1, PROMPT_T_TILE, zeros(B, HIST_FFN, FF), *fw)
        outs["fc_p"].append(_unpad_hist(fb, CONV_FF - 1, 1))
        hs, fb = _ffn_layer(hs, jnp.swapaxes(p_sample[i], 0, 1), 0, Ts, _pad_hist(state_ffn_conv[i], HIST_FFN, 0), *fw)
        outs["fc_s"].append(_unpad_hist(fb, CONV_FF - 1, 0))
    st = lambda n: jnp.stack(outs[n])
    return (hp, jnp.swapaxes(hs, 0, 1), st("rc_p"), st("rc_s"), st("rh_p"), st("rh_s"), st("pl_p"), st("pl_s"),
            st("k_p"), st("k_s"), st("v_p"), st("v_s"), st("ki_p"), st("ki_s"), st("fc_p"), st("fc_s"))
```

```python
import functools
import math

import jax
import jax.numpy as jnp
from jax import lax
from jax.experimental import pallas as pl
from jax.experimental.pallas import tpu as pltpu

F32 = jnp.float32
BF16 = jnp.bfloat16
EPS = 1e-6
LRU_C = 8.0
POOL_WINDOWS = (2, 4, 8, 16)
REC_HEADS = 8
CONV_REC = 4
CONV_FF = 3
POOL_BUF = max(POOL_WINDOWS) - 1
N_HEADS = 8
HEAD_DIM = 128
N_KV_HEADS = 4
IDX_HEADS = 8
IDX_DIM = 64
TOPK_MAX = 256
ROPE_THETA = 10000.0
PAGE_SIZE = 128

LANES = 128
SUBLANES = 8
VMEM_LIMIT = 56 * 1024 * 1024

HIST_CONV = SUBLANES
HIST_POOL = 2 * SUBLANES
HIST_FFN = SUBLANES


def _rms(x, g):
    return x * lax.rsqrt(jnp.mean(x * x, axis=-1, keepdims=True) + EPS) * g


def _dot(a, b):
    return jnp.dot(a.astype(BF16), b.astype(BF16), preferred_element_type=F32)


def _dot_nt(a, b):
    return lax.dot_general(a.astype(BF16), b.astype(BF16), (((1,), (1,)), ((), ())),
                           preferred_element_type=F32)


def _softplus(x):
    return jnp.maximum(x, 0.0) + jnp.log1p(jnp.exp(-jnp.abs(x)))


def _expm1(x):
    u = jnp.exp(x)
    near = (u - 1.0) * x / jnp.log(u)
    return jnp.where(u == 1.0, x, jnp.where(jnp.abs(x) > 0.5, u - 1.0, near))


def _tix(ta, start, size):
    t = slice(start, start + size) if isinstance(start, int) else pl.ds(start, size)
    return (slice(None), t, slice(None)) if ta == 1 else (t, slice(None), slice(None))


def _const_spec(shape):
    nd = len(shape)
    return pl.BlockSpec(shape, lambda *_: (0,) * nd, pipeline_mode=pl.Buffered(1))


def _tile_spec(shape, ta):
    if ta == 1:
        return pl.BlockSpec(shape, lambda i: (0, i, 0))
    return pl.BlockSpec(shape, lambda i: (i, 0, 0))


def _rec_body(ta, pos0, h_ref, g_ref, win_ref, cw_ref, cb_ref, wr_ref, br_ref, wi_ref, bi_ref,
              lam_ref, wp_ref, ps_ref, wout_ref, conv0_ref, h0_ref, pool0_ref,
              out_ref, conv_out_ref, hlast_ref, pool_out_ref,
              xa_ext, xb_ext, a_s, u_s, hs_s, hc_s):
    step = pl.program_id(0)
    A, S, D = h_ref.shape
    R = A * S
    n_t = S if ta == 1 else A
    C = xa_ext.shape[-1]

    @pl.when(step == 0)
    def _():
        xa_ext[_tix(ta, 0, HIST_CONV)] = conv0_ref[...]
        xb_ext[_tix(ta, 0, HIST_POOL)] = pool0_ref[...]
        hc_s[...] = h0_ref[...]

    x = h_ref[...].reshape(R, D)
    z = _dot(_rms(x, g_ref[...]), win_ref[...])
    ga = z[:, C:2 * C]
    xa_ext[_tix(ta, HIST_CONV, n_t)] = z[:, :C].reshape(A, S, C)
    xb_ext[_tix(ta, HIST_POOL, n_t)] = z[:, 2 * C:].reshape(A, S, C)

    xc = cb_ref[...][None]
    for k in range(CONV_REC):
        xc = xc + xa_ext[_tix(ta, HIST_CONV - (CONV_REC - 1) + k, n_t)] * cw_ref[k:k + 1, :][None]
    xc = xc.reshape(R, C)

    r = jax.nn.sigmoid(_dot(xc, wr_ref[...]) + br_ref[...])
    gi = jax.nn.sigmoid(_dot(xc, wi_ref[...]) + bi_ref[...])
    log_a = (-LRU_C * r) * _softplus(-lam_ref[...])
    a_s[...] = jnp.exp(log_a).reshape(A, S, C)
    u_s[...] = (jnp.sqrt(-_expm1(2.0 * log_a)) * gi * xc).reshape(A, S, C)

    def scan_step(t, h):
        ix = _tix(ta, t, 1)
        h = a_s[ix] * h + u_s[ix]
        hs_s[ix] = h
        return h

    if n_t <= SUBLANES:
        h = hc_s[...]
        for t in range(n_t):
            h = scan_step(t, h)
    else:
        h = lax.fori_loop(0, n_t, scan_step, hc_s[...], unroll=SUBLANES)
    hc_s[...] = h
    hlast_ref[...] = h
    ya = hs_s[...].reshape(R, C) * jax.nn.gelu(ga)

    G = C // len(POOL_WINDOWS)
    pos = pos0 + step * n_t + lax.broadcasted_iota(jnp.int32, (A, S, G), ta)
    posf = pos.astype(F32) + 1.0
    ds = []
    for gidx, w in enumerate(POOL_WINDOWS):
        ln = slice(gidx * G, (gidx + 1) * G)
        cur = xb_ext[_tix(ta, HIST_POOL, n_t)[:2] + (ln,)]
        acc = cur
        for j in range(1, w):
            acc = acc + xb_ext[_tix(ta, HIST_POOL - j, n_t)[:2] + (ln,)]
        ds.append(acc / jnp.minimum(jnp.float32(w), posf) - cur)
    d = jnp.concatenate(ds, axis=-1).reshape(R, C)
    yb = _dot(d, wp_ref[...]) * ps_ref[...]

    y = _dot(ya, wout_ref[:C, :]) + _dot(yb, wout_ref[C:, :])
    out_ref[...] = (x + y).reshape(A, S, D)

    new_conv = xa_ext[_tix(ta, n_t, HIST_CONV)]
    xa_ext[_tix(ta, 0, HIST_CONV)] = new_conv
    conv_out_ref[...] = new_conv
    new_pool = xb_ext[_tix(ta, n_t, HIST_POOL)]
    xb_ext[_tix(ta, 0, HIST_POOL)] = new_pool
    pool_out_ref[...] = new_pool


def _block_diag(w):
    H, a, b = w.shape
    eye = jnp.eye(H, dtype=w.dtype)
    return (eye[:, None, :, None] * w[:, :, None, :]).reshape(H * a, H * b)


def _rec_layer(h, ta, t_tile, pos0, conv0, h0, pool0, norm_g, w_in, conv_w, conv_b,
               w_r, b_r, w_i, b_i, lam, w_pool, pool_scale, w_out):
    D = h.shape[-1]
    C = conv_w.shape[-1]
    n_time = h.shape[ta]
    n_batch = h.shape[1 - ta]
    assert n_time % t_tile == 0
    if ta == 1:
        blk = (n_batch, t_tile, D)
        ext = lambda hist: (n_batch, hist + t_tile, C)
        hist_shape = lambda hist: (n_batch, hist, C)
        row_shape = (n_batch, 1, C)
        tile_c = (n_batch, t_tile, C)
    else:
        blk = (t_tile, n_batch, D)
        ext = lambda hist: (hist + t_tile, n_batch, C)
        hist_shape = lambda hist: (hist, n_batch, C)
        row_shape = (1, n_batch, C)
        tile_c = (t_tile, n_batch, C)
    row = lambda v: v.reshape(1, -1)
    args = (h, row(norm_g), w_in.astype(BF16), conv_w, row(conv_b),
            _block_diag(w_r).astype(BF16), row(b_r), _block_diag(w_i).astype(BF16), row(b_i),
            row(lam), _block_diag(w_pool).astype(BF16), row(pool_scale), w_out.astype(BF16),
            conv0, h0, pool0)
    in_specs = [_tile_spec(blk, ta)] + [_const_spec(a.shape) for a in args[1:]]
    out_shape = (jax.ShapeDtypeStruct(h.shape, F32),
                 jax.ShapeDtypeStruct(hist_shape(HIST_CONV), F32),
                 jax.ShapeDtypeStruct(row_shape, F32),
                 jax.ShapeDtypeStruct(hist_shape(HIST_POOL), F32))
    out_specs = (_tile_spec(blk, ta),
                 pl.BlockSpec(hist_shape(HIST_CONV), lambda i: (0, 0, 0)),
                 pl.BlockSpec(row_shape, lambda i: (0, 0, 0)),
                 pl.BlockSpec(hist_shape(HIST_POOL), lambda i: (0, 0, 0)))
    scratch = [pltpu.VMEM(ext(HIST_CONV), F32), pltpu.VMEM(ext(HIST_POOL), F32),
               pltpu.VMEM(tile_c, F32), pltpu.VMEM(tile_c, F32), pltpu.VMEM(tile_c, F32),
               pltpu.VMEM(row_shape, F32)]
    return pl.pallas_call(
        functools.partial(_rec_body, ta, pos0),
        grid=(n_time // t_tile,),
        in_specs=in_specs, out_specs=out_specs, out_shape=out_shape, scratch_shapes=scratch,
        compiler_params=pltpu.CompilerParams(dimension_semantics=("arbitrary",),
                                             vmem_limit_bytes=VMEM_LIMIT),
        name="rec_pool_mixer",
    )(*args)


def _ffn_body(ta, n_chunks, h_ref, p_ref, gf_ref, wup_ref, cw_ref, cb_ref, wdn_ref, gp_ref,
              wple_ref, wpg_ref, buf0_ref, out_ref, buf_out_ref, g_ext):
    step = pl.program_id(0)
    A, S, D = h_ref.shape
    R = A * S
    n_t = S if ta == 1 else A
    FF = g_ext.shape[-1]
    ch = FF // n_chunks

    @pl.when(step == 0)
    def _():
        g_ext[_tix(ta, 0, HIST_FFN)] = buf0_ref[...]

    x = h_ref[...].reshape(R, D)
    xn = _rms(x, gf_ref[...]).astype(BF16)
    acc = jnp.zeros((R, D), F32)
    for c in range(n_chunks):
        ln = slice(c * ch, (c + 1) * ch)
        g = jnp.dot(xn, wup_ref[:, ln], preferred_element_type=F32)
        u = jnp.dot(xn, wup_ref[:, FF + c * ch:FF + (c + 1) * ch], preferred_element_type=F32)
        g_ext[_tix(ta, HIST_FFN, n_t)[:2] + (ln,)] = g.reshape(A, S, ch)
        gc = cb_ref[:, ln][None]
        for k in range(CONV_FF):
            gc = gc + (g_ext[_tix(ta, HIST_FFN - (CONV_FF - 1) + k, n_t)[:2] + (ln,)]
                       * cw_ref[k:k + 1, ln][None])
        act = jax.nn.gelu(gc).reshape(R, ch) * u
        acc = acc + _dot(act, wdn_ref[ln, :])
    h1 = x + acc
    gate = jax.nn.sigmoid(_dot(_rms(h1, gp_ref[...]), wpg_ref[...]))
    pp = _dot(p_ref[...].reshape(R, p_ref.shape[-1]), wple_ref[...])
    out_ref[...] = (h1 + pp * gate).reshape(A, S, D)

    new_buf = g_ext[_tix(ta, n_t, HIST_FFN)]
    g_ext[_tix(ta, 0, HIST_FFN)] = new_buf
    buf_out_ref[...] = new_buf


def _ffn_layer(h, p, ta, t_tile, buf0, norm_f, w_up, conv_w, conv_b, w_down, norm_p, w_ple, w_pg,
               n_chunks=2):
    D = h.shape[-1]
    FF = conv_w.shape[-1]
    n_time = h.shape[ta]
    n_batch = h.shape[1 - ta]
    assert n_time % t_tile == 0 and FF % (n_chunks * LANES) == 0
    if ta == 1:
        blk = lambda c: (n_batch, t_tile, c)
        ext = (n_batch, HIST_FFN + t_tile, FF)
        hist = (n_batch, HIST_FFN, FF)
    else:
        blk = lambda c: (t_tile, n_batch, c)
        ext = (HIST_FFN + t_tile, n_batch, FF)
        hist = (HIST_FFN, n_batch, FF)
    row = lambda v: v.reshape(1, -1)
    args = (h, p, row(norm_f), w_up.astype(BF16), conv_w, row(conv_b), w_down.astype(BF16),
            row(norm_p), w_ple.astype(BF16), w_pg.astype(BF16), buf0)
    in_specs = ([_tile_spec(blk(D), ta), _tile_spec(blk(p.shape[-1]), ta)]
                + [_const_spec(a.shape) for a in args[2:]])
    out_shape = (jax.ShapeDtypeStruct(h.shape, F32), jax.ShapeDtypeStruct(hist, F32))
    out_specs = (_tile_spec(blk(D), ta), pl.BlockSpec(hist, lambda i: (0, 0, 0)))
    return pl.pallas_call(
        functools.partial(_ffn_body, ta, n_chunks),
        grid=(n_time // t_tile,),
        in_specs=in_specs, out_specs=out_specs, out_shape=out_shape,
        scratch_shapes=[pltpu.VMEM(ext, F32)],
        compiler_params=pltpu.CompilerParams(dimension_semantics=("arbitrary",),
                                             vmem_limit_bytes=VMEM_LIMIT),
        name="conv_ffn_ple",
    )(*args)


def _pad_hist(state, hist, ta):
    n = state.shape[1]
    if ta == 1:
        return jnp.pad(state, ((0, 0), (hist - n, 0), (0, 0)))
    return jnp.pad(jnp.swapaxes(state, 0, 1), ((hist - n, 0), (0, 0), (0, 0)))


def _unpad_hist(block, n, ta):
    if ta == 1:
        return block[:, block.shape[1] - n:, :]
    return jnp.swapaxes(block[block.shape[0] - n:], 0, 1)


Q_OFF = 0
K_OFF = N_HEADS * HEAD_DIM
V_OFF = K_OFF + N_KV_HEADS * HEAD_DIM
QI_OFF = V_OFF + N_KV_HEADS * HEAD_DIM
KW_OFF = QI_OFF + IDX_HEADS * IDX_DIM
ATTN_COLS = KW_OFF + LANES
IDX_SCALE = (IDX_DIM ** -0.5) * (IDX_HEADS ** -0.5)
ATTN_SCALE = HEAD_DIM ** -0.5


def _rope_tables(pos, dim):
    half = dim // 2
    inv = jnp.power(ROPE_THETA, -jnp.arange(half, dtype=F32) / half)
    ang = pos.astype(F32)[:, None] * inv[None, :]
    cos, sin = jnp.cos(ang), jnp.sin(ang)
    reps = LANES // dim
    return (jnp.tile(jnp.concatenate([cos, cos], axis=-1), (1, reps)),
            jnp.tile(jnp.concatenate([-sin, sin], axis=-1), (1, reps)))


def _proj_body(n_stacked, x_ref, g_ref, w_ref, qn_ref, kn_ref, cq_ref, sq_ref, ci_ref, si_ref, *refs):
    q_ref, k_ref, v_ref, kb_ref, vb_ref, qi_ref, kw_ref = refs[n_stacked:]
    R = x_ref.shape[0]
    z = _dot(_rms(x_ref[...], g_ref[...]), w_ref[...])
    cq, sq, ci, si = cq_ref[...], sq_ref[...], ci_ref[...], si_ref[...]
    lane = lax.broadcasted_iota(jnp.int32, cq.shape, 1)
    low_half = (lane % IDX_DIM) < (IDX_DIM // 2)

    def rope_head(t):
        return t * cq + pltpu.roll(t, HEAD_DIM // 2, 1) * sq

    def rope_idx(t):
        partner = jnp.where(low_half, pltpu.roll(t, LANES - IDX_DIM // 2, 1), pltpu.roll(t, IDX_DIM // 2, 1))
        return t * ci + partner * si

    def head_norm(t, g):
        return t * lax.rsqrt(jnp.mean(t * t, axis=-1, keepdims=True) + EPS) * g

    for h in range(N_HEADS):
        ln = slice(h * HEAD_DIM, (h + 1) * HEAD_DIM)
        q_ref[:, ln] = (rope_head(head_norm(z[:, ln], qn_ref[...])) * ATTN_SCALE).astype(BF16)
    for h in range(N_KV_HEADS):
        ln = slice(h * HEAD_DIM, (h + 1) * HEAD_DIM)
        kh = rope_head(head_norm(z[:, K_OFF + h * HEAD_DIM:K_OFF + (h + 1) * HEAD_DIM], kn_ref[...]))
        vh = z[:, V_OFF + h * HEAD_DIM:V_OFF + (h + 1) * HEAD_DIM]
        k_ref[pl.ds(h, R, stride=N_KV_HEADS), :] = kh
        v_ref[pl.ds(h, R, stride=N_KV_HEADS), :] = vh
        kb_ref[:, ln] = kh.astype(BF16)
        vb_ref[:, ln] = vh.astype(BF16)
    for s in range(IDX_HEADS * IDX_DIM // LANES):
        ln = slice(s * LANES, (s + 1) * LANES)
        qi_ref[:, ln] = rope_idx(z[:, QI_OFF + s * LANES:QI_OFF + (s + 1) * LANES]).astype(BF16)
    kw = z[:, KW_OFF:]
    kw_ref[...] = jnp.where(lane < IDX_DIM, rope_idx(kw), kw * IDX_SCALE)


def _attn_project(x, pos_tab, r_tile, layer, n_layers, kv_stacked, norm_g, w_in, q_norm, k_norm):
    N, D = x.shape
    P = pos_tab.shape[0]
    assert N % r_tile == 0 and P % r_tile == 0
    w = jnp.pad(w_in, ((0, 0), (0, ATTN_COLS - w_in.shape[1]))).astype(BF16)
    cq, sq = _rope_tables(pos_tab, HEAD_DIM)
    ci, si = _rope_tables(pos_tab, IDX_DIM)
    n_tab = P // r_tile
    row_spec = lambda c: pl.BlockSpec((r_tile, c), lambda i: (i, 0))
    tab_spec = pl.BlockSpec((r_tile, LANES), lambda i: (i % n_tab, 0))
    stack_spec = pl.BlockSpec((None, r_tile * N_KV_HEADS, HEAD_DIM), lambda i: (layer, i, 0))
    stack_shape = jax.ShapeDtypeStruct((n_layers, N * N_KV_HEADS, HEAD_DIM), F32)
    row = lambda v: v.reshape(1, -1)
    kv_w = N_KV_HEADS * HEAD_DIM
    rows_out = lambda c, dt: (row_spec(c), jax.ShapeDtypeStruct((N, c), dt))
    outs = (rows_out(N_HEADS * HEAD_DIM, BF16), (stack_spec, stack_shape), (stack_spec, stack_shape),
            rows_out(kv_w, BF16), rows_out(kv_w, BF16), rows_out(IDX_HEADS * IDX_DIM, BF16), rows_out(LANES, F32))
    in_specs = [row_spec(D), _const_spec((1, D)), _const_spec(w.shape), _const_spec((1, HEAD_DIM)),
                _const_spec((1, HEAD_DIM)), tab_spec, tab_spec, tab_spec, tab_spec]
    if kv_stacked is None:
        kv_stacked = (jnp.zeros(stack_shape.shape, F32), jnp.zeros(stack_shape.shape, F32))
    stacked = tuple(kv_stacked)
    aliases = {len(in_specs) + n: 1 + n for n in range(len(stacked))}
    in_specs += [pl.BlockSpec(memory_space=pl.ANY)] * len(stacked)
    return pl.pallas_call(
        functools.partial(_proj_body, len(stacked)),
        grid=(N // r_tile,),
        in_specs=in_specs,
        out_specs=tuple(s for s, _ in outs),
        out_shape=tuple(s for _, s in outs),
        input_output_aliases=aliases,
        compiler_params=pltpu.CompilerParams(dimension_semantics=("arbitrary",),
                                             vmem_limit_bytes=VMEM_LIMIT),
        name="attn_project",
    )(x, row(norm_g), w, row(q_norm), row(k_norm), cq, sq, ci, si, *stacked)


KEY_POS_INF = 0x7F800000
KEY_NEG_INF = -0x7F800001
REFINE_STEPS = 8


def _key_to_float(key):
    bits = key ^ (jnp.right_shift(key, 31) & jnp.int32(0x7FFFFFFF))
    return lax.bitcast_convert_type(bits, F32)


def _topk_bias(segs, k, tri_ref):
    kf = jnp.float32(k)

    def count_ge(t):
        tot = None
        for x, _ in segs:
            c = jnp.sum(jnp.where(x >= t, 1.0, 0.0), axis=-1, keepdims=True)
            tot = c if tot is None else tot + c
        return tot

    def enough(cand):
        ok = (cand < KEY_NEG_INF) | (count_ge(_key_to_float(cand)) >= kf)
        return ok & (cand <= KEY_POS_INF)

    zero = jnp.zeros((segs[0][0].shape[0], 1), jnp.int32)
    r = jnp.where(enough(zero), zero, jnp.int32(-2 ** 31))

    def bit_step(i, r):
        cand = r | jnp.left_shift(jnp.int32(1), 30 - i)
        return jnp.where(enough(cand), cand, r)

    r = lax.fori_loop(0, 31, bit_step, r)
    lo = _key_to_float(r)
    hi = jnp.where(r >= KEY_POS_INF, jnp.inf, _key_to_float(r + 1))

    def refine(_, lh):
        lo, hi = lh
        mid = lo * 0.5 + hi * 0.5
        up = count_ge(mid) >= kf
        return jnp.where(up, mid, lo), jnp.where(up, hi, mid)

    lo, hi = lax.fori_loop(0, REFINE_STEPS, refine, (lo, hi))
    need = kf - count_ge(hi)
    off = jnp.zeros_like(need)
    out = []
    for x, valid_fn in segs:
        n = x.shape[-1]
        w = min(n, LANES)
        pieces = []
        for c in range(n // w):
            xc = x[:, c * w:(c + 1) * w]
            above = jnp.where(xc >= hi, 1.0, 0.0)
            tie = jnp.where(xc >= lo, 1.0, 0.0) - above
            pref = jnp.dot(tie.astype(BF16), tri_ref[:w, :w], preferred_element_type=F32)
            take = above + tie * jnp.where(off + pref <= need, 1.0, 0.0)
            pieces.append(jnp.where(valid_fn(c * w, w), jnp.where(take > 0.5, 0.0, -jnp.inf), -jnp.inf))
            off = off + pref[:, w - 1:w]
        out.append(pieces[0] if len(pieces) == 1 else jnp.concatenate(pieces, axis=-1))
    return out


def _dsa_prompt_body(topk, tile0, q_ref, qi_ref, kwq_ref, k_ref, v_ref, kwk_ref, h_ref, wout_ref, tri_ref, out_ref):
    Tq = q_ref.shape[0]
    S = k_ref.shape[0]
    grp = N_HEADS // N_KV_HEADS
    qpos0 = (tile0 + pl.program_id(1)) * Tq

    def causal(c0, w):
        qpos = qpos0 + lax.broadcasted_iota(jnp.int32, (Tq, w), 0)
        return c0 + lax.broadcasted_iota(jnp.int32, (Tq, w), 1) <= qpos

    ki = kwk_ref[:, :IDX_DIM].astype(BF16)
    wq = kwq_ref[...]
    scores = None
    for h in range(IDX_HEADS):
        s = _dot_nt(qi_ref[:, h * IDX_DIM:(h + 1) * IDX_DIM], ki)
        t = jnp.maximum(s, 0.0) * wq[:, IDX_DIM + h:IDX_DIM + h + 1]
        scores = t if scores is None else scores + t
    scores = jnp.where(causal(0, S), scores, -jnp.inf)
    (bias,) = _topk_bias([(scores, causal)], topk, tri_ref)

    outs = []
    for g in range(N_KV_HEADS):
        ln = slice(g * HEAD_DIM, (g + 1) * HEAD_DIM)
        qs = jnp.concatenate([q_ref[:, (g * grp + hh) * HEAD_DIM:(g * grp + hh + 1) * HEAD_DIM] for hh in range(grp)],
                             axis=0)
        s = _dot_nt(qs, k_ref[:, ln])
        ps, ls = [], []
        for hh in range(grp):
            sh = s[hh * Tq:(hh + 1) * Tq] + bias
            p = jnp.exp(sh - jnp.max(sh, axis=-1, keepdims=True))
            ls.append(jnp.sum(p, axis=-1, keepdims=True))
            ps.append(p.astype(BF16))
        o = _dot(jnp.concatenate(ps, axis=0), v_ref[:, ln])
        outs += [o[hh * Tq:(hh + 1) * Tq] / ls[hh] for hh in range(grp)]
    out_ref[...] = h_ref[...] + _dot(jnp.concatenate(outs, axis=-1), wout_ref[...])


def _tri(n):
    i = jnp.arange(n)
    return (i[:, None] <= i[None, :]).astype(BF16)


def _dsa_prompt(h, q, k, v, qi, kw, w_out, q_tile):
    B, T, D = h.shape
    topk = min(TOPK_MAX, T // 4)
    wb, tri = w_out.astype(BF16), _tri(LANES)
    for tile in range(T // q_tile):
        S = (tile + 1) * q_tile
        qspec = lambda c, tile=tile: pl.BlockSpec((None, q_tile, c), lambda b, j: (b, tile + j, 0))
        kspec = lambda c, S=S: pl.BlockSpec((None, S, c), lambda b, j: (b, 0, 0))
        h = pl.pallas_call(
            functools.partial(_dsa_prompt_body, topk, tile),
            grid=(B, 1),
            in_specs=[qspec(q.shape[-1]), qspec(qi.shape[-1]), qspec(LANES), kspec(k.shape[-1]), kspec(v.shape[-1]),
                      kspec(LANES), qspec(D), _const_spec(wb.shape), _const_spec((LANES, LANES))],
            out_specs=qspec(D),
            out_shape=jax.ShapeDtypeStruct(h.shape, F32),
            input_output_aliases={6: 0},
            compiler_params=pltpu.CompilerParams(dimension_semantics=("arbitrary", "arbitrary"),
                                                 vmem_limit_bytes=VMEM_LIMIT),
            name="dsa_prompt",
        )(q, qi, kw, k, v, kw, h, wb, tri)
    return h


PAGES_PER_CHUNK = 16
Q_ROWS = SUBLANES


def _sample_index_body(layer, n_pages, pt_ref, qi_ref, wq_ref, kinew_ref, ckidx_ref, out_ref, kidx_buf, sems):
    b = pl.program_id(0)
    nb = pl.num_programs(0)
    past = n_pages * PAGE_SIZE
    n_tok = out_ref.shape[0]

    def kidx_copy(bb, p):
        return pltpu.make_async_copy(ckidx_ref.at[layer, pt_ref[bb, p]], kidx_buf.at[bb % 2, p], sems.at[bb % 2])

    def start_kidx(bb):
        for p in range(n_pages):
            kidx_copy(bb, p).start()

    @pl.when(b == 0)
    def _():
        start_kidx(b)

    for p in range(n_pages):
        kidx_copy(b, p).wait()

    @pl.when(b + 1 < nb)
    def _():
        start_kidx(b + 1)

    qi = qi_ref[...]
    wq = wq_ref[...]

    def head_sum(s):
        return jnp.sum((jnp.maximum(s, 0.0) * wq).reshape(n_tok, IDX_HEADS, s.shape[-1]), axis=1)

    for p in range(n_pages):
        out_ref[:, p * PAGE_SIZE:(p + 1) * PAGE_SIZE] = head_sum(_dot(qi, kidx_buf[b % 2, p]))
    i_new = head_sum(_dot_nt(qi, kinew_ref[:, :IDX_DIM]))
    tq = lax.broadcasted_iota(jnp.int32, i_new.shape, 0)
    out_ref[:, past:] = jnp.where(lax.broadcasted_iota(jnp.int32, i_new.shape, 1) <= tq, i_new, -jnp.inf)


def _sample_index_scores(qi, kw, cache_kidx, page_table, layer):
    B, n_tok, _ = qi.shape
    n_pages = page_table.shape[1]
    rows = n_tok * IDX_HEADS
    wq_rows = kw[:, :, IDX_DIM:IDX_DIM + IDX_HEADS].reshape(B, rows, 1)
    ki_new = jnp.pad(kw, ((0, 0), (0, LANES - n_tok), (0, 0)))
    cki = jnp.swapaxes(cache_kidx, 2, 3)
    bspec = lambda *s: pl.BlockSpec((None,) + s, lambda b, pt: (b,) + (0,) * len(s))
    grid_spec = pltpu.PrefetchScalarGridSpec(
        num_scalar_prefetch=1,
        grid=(B,),
        in_specs=[bspec(rows, IDX_DIM), bspec(rows, 1), bspec(LANES, LANES), pl.BlockSpec(memory_space=pl.ANY)],
        out_specs=bspec(n_tok, n_pages * PAGE_SIZE + LANES),
        scratch_shapes=[pltpu.VMEM((2, n_pages, IDX_DIM, PAGE_SIZE), F32), pltpu.SemaphoreType.DMA((2,))],
    )
    return pl.pallas_call(
        functools.partial(_sample_index_body, layer, n_pages),
        grid_spec=grid_spec,
        out_shape=jax.ShapeDtypeStruct((B, n_tok, n_pages * PAGE_SIZE + LANES), F32),
        compiler_params=pltpu.CompilerParams(dimension_semantics=("arbitrary",), vmem_limit_bytes=VMEM_LIMIT),
        name="sample_index_scores",
    )(page_table, qi.reshape(B, rows, IDX_DIM), wq_rows, ki_new, cki)


def _sample_topk_body(topk, n_tok, past, s_ref, tri_ref, out_ref):
    R = s_ref.shape[0]
    tq = lax.broadcasted_iota(jnp.int32, (R, LANES), 0) % n_tok

    def valid_past(c0, w):
        return lax.broadcasted_iota(jnp.int32, (R, w), 1) >= 0

    def valid_new(c0, w):
        return c0 + lax.broadcasted_iota(jnp.int32, (R, w), 1) <= tq[:, :w]

    bias_past, bias_new = _topk_bias([(s_ref[:, :past], valid_past), (s_ref[:, past:], valid_new)], topk, tri_ref)
    out_ref[:, :past] = bias_past
    out_ref[:, past:] = bias_new


def _sample_topk_bias(scores, n_tok, topk):
    R, n = scores.shape
    return pl.pallas_call(
        functools.partial(_sample_topk_body, topk, n_tok, n - LANES),
        grid=(1,),
        in_specs=[_const_spec((R, n)), _const_spec((LANES, LANES))],
        out_specs=pl.BlockSpec((R, n), lambda i: (0, 0)),
        out_shape=jax.ShapeDtypeStruct((R, n), F32),
        compiler_params=pltpu.CompilerParams(dimension_semantics=("arbitrary",), vmem_limit_bytes=VMEM_LIMIT),
        name="sample_topk_bias",
    )(scores, _tri(LANES))


def _sample_attend_body(layer, n_pages, pt_ref, q8_ref, knew_ref, vnew_ref, bias_ref, h_ref, wout_ref,
                        ck_ref, cv_ref, out_ref, kv_buf, sc_ref, sems):
    b = pl.program_id(0)
    nb = pl.num_programs(0)
    n_chunks = n_pages // PAGES_PER_CHUNK
    n_loads = 2 * n_chunks
    past = n_pages * PAGE_SIZE
    page_rows = PAGE_SIZE * N_KV_HEADS
    ch_keys = PAGES_PER_CHUNK * PAGE_SIZE
    n_tok = h_ref.shape[0]

    def kv_copy(bb, i, p):
        src = ck_ref if i < n_chunks else cv_ref
        page = pt_ref[bb, (i % n_chunks) * PAGES_PER_CHUNK + p]
        return pltpu.make_async_copy(src.at[layer, page], kv_buf.at[i % 2, pl.ds(p * page_rows, page_rows)],
                                     sems.at[i % 2])

    def start_load(bb, i):
        for p in range(PAGES_PER_CHUNK):
            kv_copy(bb, i, p).start()

    def wait_load(i):
        for p in range(PAGES_PER_CHUNK):
            kv_copy(b, i, p).wait()

    def start_after(i):
        if i + 2 < n_loads:
            start_load(b, i + 2)
        else:
            @pl.when(b + 1 < nb)
            def _():
                start_load(b + 1, i + 2 - n_loads)

    def head_rows(slot, g):
        return kv_buf[slot, pl.ds(g, ch_keys, stride=N_KV_HEADS), :]

    @pl.when(b == 0)
    def _():
        start_load(b, 0)
        start_load(b, 1)

    bias_new = bias_ref[:, past:past + Q_ROWS]

    for c in range(n_chunks):
        wait_load(c)
        ks = slice(c * ch_keys, (c + 1) * ch_keys)
        for g in range(N_KV_HEADS):
            sc_ref[g, :, ks] = _dot_nt(q8_ref[g], head_rows(c % 2, g)) + bias_ref[:, ks]
        start_after(c)

    acc, denom = [], []
    for g in range(N_KV_HEADS):
        ln = slice(g * HEAD_DIM, (g + 1) * HEAD_DIM)
        s_new = _dot_nt(q8_ref[g], knew_ref[:, ln]) + bias_new
        s_past = sc_ref[g]
        m = jnp.maximum(jnp.max(s_past, axis=-1, keepdims=True), jnp.max(s_new, axis=-1, keepdims=True))
        p_past = jnp.exp(s_past - m)
        p_new = jnp.exp(s_new - m)
        sc_ref[g] = p_past
        denom.append(jnp.sum(p_past, axis=-1, keepdims=True) + jnp.sum(p_new, axis=-1, keepdims=True))
        acc.append(_dot(p_new, vnew_ref[:, ln]))

    for c in range(n_chunks):
        i = n_chunks + c
        wait_load(i)
        ks = slice(c * ch_keys, (c + 1) * ch_keys)
        for g in range(N_KV_HEADS):
            acc[g] = acc[g] + _dot(sc_ref[g, :, ks], head_rows(i % 2, g))
        start_after(i)

    heads = []
    for g in range(N_KV_HEADS):
        og = acc[g] / denom[g]
        for hh in range(N_HEADS // N_KV_HEADS):
            heads.append(og[hh * n_tok:(hh + 1) * n_tok])
    o = jnp.concatenate(heads, axis=-1)
    out_ref[...] = h_ref[...] + _dot(o, wout_ref[...])


def _dsa_sample(h, q, k, v, qi, kw, cache_k, cache_v, cache_kidx, page_table, layer, w_out):
    B, n_tok, D = h.shape
    grp = N_HEADS // N_KV_HEADS
    assert grp * n_tok == Q_ROWS
    n_pages = page_table.shape[1]
    assert n_pages % PAGES_PER_CHUNK == 0
    past = n_pages * PAGE_SIZE
    topk = min(TOPK_MAX, (past + n_tok) // 4)
    scores = _sample_index_scores(qi, kw, cache_kidx, page_table, layer)
    bias = _sample_topk_bias(scores.reshape(B * n_tok, past + LANES), n_tok, topk).reshape(scores.shape)
    bias = jnp.concatenate([bias] * grp, axis=1)
    q8 = q.reshape(B, n_tok, N_KV_HEADS, grp, HEAD_DIM).transpose(0, 2, 3, 1, 4).reshape(B, N_KV_HEADS, Q_ROWS, HEAD_DIM)
    pad8 = lambda a: jnp.pad(a, ((0, 0), (0, Q_ROWS - n_tok), (0, 0)))
    ck = cache_k.reshape(cache_k.shape[0], cache_k.shape[1], PAGE_SIZE * N_KV_HEADS, HEAD_DIM)
    cv = cache_v.reshape(ck.shape)
    bspec = lambda *s: pl.BlockSpec((None,) + s, lambda b, pt: (b,) + (0,) * len(s))
    any_spec = pl.BlockSpec(memory_space=pl.ANY)
    grid_spec = pltpu.PrefetchScalarGridSpec(
        num_scalar_prefetch=1,
        grid=(B,),
        in_specs=[bspec(N_KV_HEADS, Q_ROWS, HEAD_DIM), bspec(Q_ROWS, k.shape[-1]), bspec(Q_ROWS, v.shape[-1]),
                  bspec(Q_ROWS, past + LANES), bspec(n_tok, D),
                  pl.BlockSpec(w_out.shape, lambda b, pt: (0, 0), pipeline_mode=pl.Buffered(1)),
                  any_spec, any_spec],
        out_specs=bspec(n_tok, D),
        scratch_shapes=[pltpu.VMEM((2, PAGES_PER_CHUNK * PAGE_SIZE * N_KV_HEADS, HEAD_DIM), F32),
                        pltpu.VMEM((N_KV_HEADS, Q_ROWS, past), F32),
                        pltpu.SemaphoreType.DMA((2,))],
    )
    return pl.pallas_call(
        functools.partial(_sample_attend_body, layer, n_pages),
        grid_spec=grid_spec,
        out_shape=jax.ShapeDtypeStruct(h.shape, F32),
        compiler_params=pltpu.CompilerParams(dimension_semantics=("arbitrary",),
                                             vmem_limit_bytes=VMEM_LIMIT),
        name="sample_attend",
    )(page_table, q8, pad8(k), pad8(v), bias, h, w_out.astype(BF16), ck, cv)


PROMPT_T_TILE = 64
PROJ_ROW_TILE = 512
PROMPT_Q_TILE = 256


def kernel(x_prompt, x_sample, cache_k, cache_v, cache_kidx, state_rec_conv, state_rec_h, state_pool, state_ffn_conv, page_table, p_prompt, p_sample, norm_mix, norm_ffn, norm_ple, w_in_rec, conv_rec_w, conv_rec_b, w_rgate, b_rgate, w_igate, b_igate, lru_lambda, w_pool, pool_scale, w_out_rec, w_in_attn, q_norm, k_norm, w_out_attn, w_up, conv_ff_w, conv_ff_b, w_down, w_ple, w_ple_gate):
    B, T, D = x_prompt.shape
    Bs, Ts, _ = x_sample.shape
    depth = norm_mix.shape[0]
    C = conv_rec_w.shape[-1]
    FF = conv_ff_w.shape[-1]
    past = page_table.shape[1] * PAGE_SIZE
    hp = x_prompt
    hs = jnp.swapaxes(x_sample, 0, 1)
    zeros = lambda *s: jnp.zeros(s, F32)
    n_attn = depth // 2
    kv_p = kv_s = None
    outs = {n: [] for n in ("rc_p", "rc_s", "rh_p", "rh_s", "pl_p", "pl_s", "ki_p", "ki_s", "fc_p", "fc_s")}
    for i in range(depth):
        j = i // 2
        if i % 2 == 0:
            wts = (norm_mix[i], w_in_rec[j], conv_rec_w[j], conv_rec_b[j], w_rgate[j], b_rgate[j],
                   w_igate[j], b_igate[j], lru_lambda[j], w_pool[j], pool_scale[j], w_out_rec[j])
            hp, c, hh, pb = _rec_layer(hp, 1, PROMPT_T_TILE, 0, zeros(B, HIST_CONV, C), zeros(B, 1, C),
                                       zeros(B, HIST_POOL, C), *wts)
            outs["rc_p"].append(_unpad_hist(c, CONV_REC - 1, 1))
            outs["rh_p"].append(hh[:, 0])
            outs["pl_p"].append(_unpad_hist(pb, POOL_BUF, 1))
            hs, c, hh, pb = _rec_layer(hs, 0, Ts, past, _pad_hist(state_rec_conv[j], HIST_CONV, 0),
                                       state_rec_h[j][None], _pad_hist(state_pool[j], HIST_POOL, 0), *wts)
            outs["rc_s"].append(_unpad_hist(c, CONV_REC - 1, 0))
            outs["rh_s"].append(hh[0])
            outs["pl_s"].append(_unpad_hist(pb, POOL_BUF, 0))
        else:
            pw = (norm_mix[i], w_in_attn[j], q_norm[j], k_norm[j])
            q, k, v, kb, vb, qi, kw = _attn_project(hp.reshape(B * T, D), jnp.arange(T), PROJ_ROW_TILE, j, n_attn,
                                                    kv_p, *pw)
            kv_p = (k, v)
            r3 = lambda a: a.reshape(B, T, -1)
            hp = _dsa_prompt(hp, r3(q), r3(kb), r3(vb), r3(qi), r3(kw), w_out_attn[j], PROMPT_Q_TILE)
            outs["ki_p"].append(r3(kw)[..., :IDX_DIM])
            pos_s = past + jnp.arange(Ts * Bs) // Bs
            q, k, v, kb, vb, qi, kw = _attn_project(hs.reshape(Ts * Bs, D), pos_s, Ts * Bs, j, n_attn, kv_s, *pw)
            kv_s = (k, v)
            bm = lambda a: jnp.swapaxes(a.reshape(Ts, Bs, -1), 0, 1)
            kw = bm(kw)
            hs = jnp.swapaxes(_dsa_sample(bm(hs), bm(q), bm(kb), bm(vb), bm(qi), kw, cache_k, cache_v, cache_kidx,
                                          page_table, j, w_out_attn[j]), 0, 1)
            outs["ki_s"].append(kw[..., :IDX_DIM])
        fw = (norm_ffn[i], w_up[i], conv_ff_w[i], conv_ff_b[i], w_down[i], norm_ple[i], w_ple[i], w_ple_gate[i])
        hp, fb = _ffn_layer(hp, p_prompt[i], 1, PROMPT_T_TILE, zeros(B, HIST_FFN, FF), *fw)
        outs["fc_p"].append(_unpad_hist(fb, CONV_FF - 1, 1))
        hs, fb = _ffn_layer(hs, jnp.swapaxes(p_sample[i], 0, 1), 0, Ts, _pad_hist(state_ffn_conv[i], HIST_FFN, 0), *fw)
        outs["fc_s"].append(_unpad_hist(fb, CONV_FF - 1, 0))
    st = lambda n: jnp.stack(outs[n])
    kv_prompt = lambda a: a.reshape(n_attn, B, T, N_KV_HEADS, HEAD_DIM)
    kv_sample = lambda a: jnp.swapaxes(a.reshape(n_attn, Ts, Bs, N_KV_HEADS, HEAD_DIM), 1, 2)
    return (hp, jnp.swapaxes(hs, 0, 1), st("rc_p"), st("rc_s"), st("rh_p"), st("rh_s"), st("pl_p"), st("pl_s"),
            kv_prompt(kv_p[0]), kv_sample(kv_s[0]), kv_prompt(kv_p[1]), kv_sample(kv_s[1]),
            st("ki_p"), st("ki_s"), st("fc_p"), st("fc_s"))
```

```python
import functools
import math

import jax
import jax.numpy as jnp
from jax import lax
from jax.experimental import pallas as pl
from jax.experimental.pallas import tpu as pltpu

F32 = jnp.float32
BF16 = jnp.bfloat16
EPS = 1e-6
LRU_C = 8.0
POOL_WINDOWS = (2, 4, 8, 16)
REC_HEADS = 8
CONV_REC = 4
CONV_FF = 3
POOL_BUF = max(POOL_WINDOWS) - 1
N_HEADS = 8
HEAD_DIM = 128
N_KV_HEADS = 4
IDX_HEADS = 8
IDX_DIM = 64
TOPK_MAX = 256
ROPE_THETA = 10000.0
PAGE_SIZE = 128

LANES = 128
SUBLANES = 8
VMEM_LIMIT = 56 * 1024 * 1024

HIST_CONV = SUBLANES
HIST_POOL = 2 * SUBLANES
HIST_FFN = SUBLANES


def _rms(x, g):
    return x * lax.rsqrt(jnp.mean(x * x, axis=-1, keepdims=True) + EPS) * g


def _dot(a, b):
    return jnp.dot(a.astype(BF16), b.astype(BF16), preferred_element_type=F32)


def _dot_nt(a, b):
    return lax.dot_general(a.astype(BF16), b.astype(BF16), (((1,), (1,)), ((), ())),
                           preferred_element_type=F32)


def _softplus(x):
    return jnp.maximum(x, 0.0) + jnp.log1p(jnp.exp(-jnp.abs(x)))


def _expm1(x):
    u = jnp.exp(x)
    near = (u - 1.0) * x / jnp.log(u)
    return jnp.where(u == 1.0, x, jnp.where(jnp.abs(x) > 0.5, u - 1.0, near))


def _tix(ta, start, size):
    t = slice(start, start + size) if isinstance(start, int) else pl.ds(start, size)
    return (slice(None), t, slice(None)) if ta == 1 else (t, slice(None), slice(None))


def _const_spec(shape):
    nd = len(shape)
    return pl.BlockSpec(shape, lambda *_: (0,) * nd, pipeline_mode=pl.Buffered(1))


def _tile_spec(shape, ta):
    if ta == 1:
        return pl.BlockSpec(shape, lambda i: (0, i, 0))
    return pl.BlockSpec(shape, lambda i: (i, 0, 0))


def _rec_body(ta, pos0, h_ref, g_ref, win_ref, cw_ref, cb_ref, wr_ref, br_ref, wi_ref, bi_ref,
              lam_ref, wp_ref, ps_ref, wout_ref, conv0_ref, h0_ref, pool0_ref,
              out_ref, conv_out_ref, hlast_ref, pool_out_ref,
              xa_ext, xb_ext, a_s, u_s, hs_s, hc_s):
    step = pl.program_id(0)
    A, S, D = h_ref.shape
    R = A * S
    n_t = S if ta == 1 else A
    C = xa_ext.shape[-1]

    @pl.when(step == 0)
    def _():
        xa_ext[_tix(ta, 0, HIST_CONV)] = conv0_ref[...]
        xb_ext[_tix(ta, 0, HIST_POOL)] = pool0_ref[...]
        hc_s[...] = h0_ref[...]

    x = h_ref[...].reshape(R, D)
    z = _dot(_rms(x, g_ref[...]), win_ref[...])
    ga = z[:, C:2 * C]
    xa_ext[_tix(ta, HIST_CONV, n_t)] = z[:, :C].reshape(A, S, C)
    xb_ext[_tix(ta, HIST_POOL, n_t)] = z[:, 2 * C:].reshape(A, S, C)

    xc = cb_ref[...][None]
    for k in range(CONV_REC):
        xc = xc + xa_ext[_tix(ta, HIST_CONV - (CONV_REC - 1) + k, n_t)] * cw_ref[k:k + 1, :][None]
    xc = xc.reshape(R, C)

    r = jax.nn.sigmoid(_dot(xc, wr_ref[...]) + br_ref[...])
    gi = jax.nn.sigmoid(_dot(xc, wi_ref[...]) + bi_ref[...])
    log_a = (-LRU_C * r) * _softplus(-lam_ref[...])
    a_s[...] = jnp.exp(log_a).reshape(A, S, C)
    u_s[...] = (jnp.sqrt(-_expm1(2.0 * log_a)) * gi * xc).reshape(A, S, C)

    def scan_step(t, h):
        ix = _tix(ta, t, 1)
        h = a_s[ix] * h + u_s[ix]
        hs_s[ix] = h
        return h

    if n_t <= SUBLANES:
        h = hc_s[...]
        for t in range(n_t):
            h = scan_step(t, h)
    else:
        h = lax.fori_loop(0, n_t, scan_step, hc_s[...], unroll=SUBLANES)
    hc_s[...] = h
    hlast_ref[...] = h
    ya = hs_s[...].reshape(R, C) * jax.nn.gelu(ga)

    G = C // len(POOL_WINDOWS)
    pos = pos0 + step * n_t + lax.broadcasted_iota(jnp.int32, (A, S, G), ta)
    posf = pos.astype(F32) + 1.0
    ds = []
    for gidx, w in enumerate(POOL_WINDOWS):
        ln = slice(gidx * G, (gidx + 1) * G)
        cur = xb_ext[_tix(ta, HIST_POOL, n_t)[:2] + (ln,)]
        acc = cur
        for j in range(1, w):
            acc = acc + xb_ext[_tix(ta, HIST_POOL - j, n_t)[:2] + (ln,)]
        ds.append(acc / jnp.minimum(jnp.float32(w), posf) - cur)
    d = jnp.concatenate(ds, axis=-1).reshape(R, C)
    yb = _dot(d, wp_ref[...]) * ps_ref[...]

    y = _dot(ya, wout_ref[:C, :]) + _dot(yb, wout_ref[C:, :])
    out_ref[...] = (x + y).reshape(A, S, D)

    new_conv = xa_ext[_tix(ta, n_t, HIST_CONV)]
    xa_ext[_tix(ta, 0, HIST_CONV)] = new_conv
    conv_out_ref[...] = new_conv
    new_pool = xb_ext[_tix(ta, n_t, HIST_POOL)]
    xb_ext[_tix(ta, 0, HIST_POOL)] = new_pool
    pool_out_ref[...] = new_pool


def _block_diag(w):
    H, a, b = w.shape
    eye = jnp.eye(H, dtype=w.dtype)
    return (eye[:, None, :, None] * w[:, :, None, :]).reshape(H * a, H * b)


def _rec_layer(h, ta, t_tile, pos0, conv0, h0, pool0, norm_g, w_in, conv_w, conv_b,
               w_r, b_r, w_i, b_i, lam, w_pool, pool_scale, w_out):
    D = h.shape[-1]
    C = conv_w.shape[-1]
    n_time = h.shape[ta]
    n_batch = h.shape[1 - ta]
    assert n_time % t_tile == 0
    if ta == 1:
        blk = (n_batch, t_tile, D)
        ext = lambda hist: (n_batch, hist + t_tile, C)
        hist_shape = lambda hist: (n_batch, hist, C)
        row_shape = (n_batch, 1, C)
        tile_c = (n_batch, t_tile, C)
    else:
        blk = (t_tile, n_batch, D)
        ext = lambda hist: (hist + t_tile, n_batch, C)
        hist_shape = lambda hist: (hist, n_batch, C)
        row_shape = (1, n_batch, C)
        tile_c = (t_tile, n_batch, C)
    row = lambda v: v.reshape(1, -1)
    args = (h, row(norm_g), w_in.astype(BF16), conv_w, row(conv_b),
            _block_diag(w_r).astype(BF16), row(b_r), _block_diag(w_i).astype(BF16), row(b_i),
            row(lam), _block_diag(w_pool).astype(BF16), row(pool_scale), w_out.astype(BF16),
            conv0, h0, pool0)
    in_specs = [_tile_spec(blk, ta)] + [_const_spec(a.shape) for a in args[1:]]
    out_shape = (jax.ShapeDtypeStruct(h.shape, F32),
                 jax.ShapeDtypeStruct(hist_shape(HIST_CONV), F32),
                 jax.ShapeDtypeStruct(row_shape, F32),
                 jax.ShapeDtypeStruct(hist_shape(HIST_POOL), F32))
    out_specs = (_tile_spec(blk, ta),
                 pl.BlockSpec(hist_shape(HIST_CONV), lambda i: (0, 0, 0)),
                 pl.BlockSpec(row_shape, lambda i: (0, 0, 0)),
                 pl.BlockSpec(hist_shape(HIST_POOL), lambda i: (0, 0, 0)))
    scratch = [pltpu.VMEM(ext(HIST_CONV), F32), pltpu.VMEM(ext(HIST_POOL), F32),
               pltpu.VMEM(tile_c, F32), pltpu.VMEM(tile_c, F32), pltpu.VMEM(tile_c, F32),
               pltpu.VMEM(row_shape, F32)]
    return pl.pallas_call(
        functools.partial(_rec_body, ta, pos0),
        grid=(n_time // t_tile,),
        in_specs=in_specs, out_specs=out_specs, out_shape=out_shape, scratch_shapes=scratch,
        compiler_params=pltpu.CompilerParams(dimension_semantics=("arbitrary",),
                                             vmem_limit_bytes=VMEM_LIMIT),
        name="rec_pool_mixer",
    )(*args)


def _ffn_body(ta, n_chunks, h_ref, p_ref, gf_ref, wup_ref, cw_ref, cb_ref, wdn_ref, gp_ref,
              wple_ref, wpg_ref, buf0_ref, out_ref, buf_out_ref, g_ext):
    step = pl.program_id(0)
    A, S, D = h_ref.shape
    R = A * S
    n_t = S if ta == 1 else A
    FF = g_ext.shape[-1]
    ch = FF // n_chunks

    @pl.when(step == 0)
    def _():
        g_ext[_tix(ta, 0, HIST_FFN)] = buf0_ref[...]

    x = h_ref[...].reshape(R, D)
    xn = _rms(x, gf_ref[...]).astype(BF16)
    acc = jnp.zeros((R, D), F32)
    for c in range(n_chunks):
        ln = slice(c * ch, (c + 1) * ch)
        g = jnp.dot(xn, wup_ref[:, ln], preferred_element_type=F32)
        u = jnp.dot(xn, wup_ref[:, FF + c * ch:FF + (c + 1) * ch], preferred_element_type=F32)
        g_ext[_tix(ta, HIST_FFN, n_t)[:2] + (ln,)] = g.reshape(A, S, ch)
        gc = cb_ref[:, ln][None]
        for k in range(CONV_FF):
            gc = gc + (g_ext[_tix(ta, HIST_FFN - (CONV_FF - 1) + k, n_t)[:2] + (ln,)]
                       * cw_ref[k:k + 1, ln][None])
        act = jax.nn.gelu(gc).reshape(R, ch) * u
        acc = acc + _dot(act, wdn_ref[ln, :])
    h1 = x + acc
    gate = jax.nn.sigmoid(_dot(_rms(h1, gp_ref[...]), wpg_ref[...]))
    pp = _dot(p_ref[...].reshape(R, p_ref.shape[-1]), wple_ref[...])
    out_ref[...] = (h1 + pp * gate).reshape(A, S, D)

    new_buf = g_ext[_tix(ta, n_t, HIST_FFN)]
    g_ext[_tix(ta, 0, HIST_FFN)] = new_buf
    buf_out_ref[...] = new_buf


def _ffn_layer(h, p, layer, ta, t_tile, buf0, norm_f, w_up, conv_w, conv_b, w_down, norm_p, w_ple, w_pg,
               n_chunks=2):
    D = h.shape[-1]
    FF = conv_w.shape[-1]
    n_time = h.shape[ta]
    n_batch = h.shape[1 - ta]
    assert n_time % t_tile == 0 and FF % (n_chunks * LANES) == 0
    if ta == 1:
        blk = lambda c: (n_batch, t_tile, c)
        p_spec = pl.BlockSpec((None,) + blk(p.shape[-1]), lambda i: (layer, 0, i, 0))
        ext = (n_batch, HIST_FFN + t_tile, FF)
        hist = (n_batch, HIST_FFN, FF)
    else:
        blk = lambda c: (t_tile, n_batch, c)
        p_spec = pl.BlockSpec((None,) + blk(p.shape[-1]), lambda i: (layer, i, 0, 0))
        ext = (HIST_FFN + t_tile, n_batch, FF)
        hist = (HIST_FFN, n_batch, FF)
    row = lambda v: v.reshape(1, -1)
    w_spec = lambda w: pl.BlockSpec((None,) + w.shape[1:], lambda i: (layer, 0, 0), pipeline_mode=pl.Buffered(1))
    args = (h, p, row(norm_f), w_up, conv_w, row(conv_b), w_down, row(norm_p), w_ple, w_pg, buf0)
    in_specs = [_tile_spec(blk(D), ta), p_spec, _const_spec((1, D)), w_spec(w_up), _const_spec(conv_w.shape),
                _const_spec((1, FF)), w_spec(w_down), _const_spec((1, D)), w_spec(w_ple), w_spec(w_pg),
                _const_spec(buf0.shape)]
    out_shape = (jax.ShapeDtypeStruct(h.shape, F32), jax.ShapeDtypeStruct(hist, F32))
    out_specs = (_tile_spec(blk(D), ta), pl.BlockSpec(hist, lambda i: (0, 0, 0)))
    return pl.pallas_call(
        functools.partial(_ffn_body, ta, n_chunks),
        grid=(n_time // t_tile,),
        in_specs=in_specs, out_specs=out_specs, out_shape=out_shape,
        scratch_shapes=[pltpu.VMEM(ext, F32)],
        compiler_params=pltpu.CompilerParams(dimension_semantics=("arbitrary",),
                                             vmem_limit_bytes=VMEM_LIMIT),
        name="conv_ffn_ple",
    )(*args)


def _pad_hist(state, hist, ta):
    n = state.shape[1]
    if ta == 1:
        return jnp.pad(state, ((0, 0), (hist - n, 0), (0, 0)))
    return jnp.pad(jnp.swapaxes(state, 0, 1), ((hist - n, 0), (0, 0), (0, 0)))


def _unpad_hist(block, n, ta):
    if ta == 1:
        return block[:, block.shape[1] - n:, :]
    return jnp.swapaxes(block[block.shape[0] - n:], 0, 1)


Q_OFF = 0
K_OFF = N_HEADS * HEAD_DIM
V_OFF = K_OFF + N_KV_HEADS * HEAD_DIM
QI_OFF = V_OFF + N_KV_HEADS * HEAD_DIM
KW_OFF = QI_OFF + IDX_HEADS * IDX_DIM
ATTN_COLS = KW_OFF + LANES
IDX_SCALE = (IDX_DIM ** -0.5) * (IDX_HEADS ** -0.5)
ATTN_SCALE = HEAD_DIM ** -0.5


def _rope_tables(pos, dim):
    half = dim // 2
    inv = jnp.power(ROPE_THETA, -jnp.arange(half, dtype=F32) / half)
    ang = pos.astype(F32)[:, None] * inv[None, :]
    cos, sin = jnp.cos(ang), jnp.sin(ang)
    reps = LANES // dim
    return (jnp.tile(jnp.concatenate([cos, cos], axis=-1), (1, reps)),
            jnp.tile(jnp.concatenate([-sin, sin], axis=-1), (1, reps)))


def _proj_body(n_stacked, x_ref, g_ref, w_ref, qn_ref, kn_ref, cq_ref, sq_ref, ci_ref, si_ref, *refs):
    q_ref, k_ref, v_ref, kb_ref, vb_ref, qi_ref, kw_ref = refs[n_stacked:]
    R = x_ref.shape[0]
    z = _dot(_rms(x_ref[...], g_ref[...]), w_ref[...])
    cq, sq, ci, si = cq_ref[...], sq_ref[...], ci_ref[...], si_ref[...]
    lane = lax.broadcasted_iota(jnp.int32, cq.shape, 1)
    low_half = (lane % IDX_DIM) < (IDX_DIM // 2)

    def rope_head(t):
        return t * cq + pltpu.roll(t, HEAD_DIM // 2, 1) * sq

    def rope_idx(t):
        partner = jnp.where(low_half, pltpu.roll(t, LANES - IDX_DIM // 2, 1), pltpu.roll(t, IDX_DIM // 2, 1))
        return t * ci + partner * si

    def head_norm(t, g):
        return t * lax.rsqrt(jnp.mean(t * t, axis=-1, keepdims=True) + EPS) * g

    for h in range(N_HEADS):
        ln = slice(h * HEAD_DIM, (h + 1) * HEAD_DIM)
        q_ref[:, ln] = (rope_head(head_norm(z[:, ln], qn_ref[...])) * ATTN_SCALE).astype(BF16)
    for h in range(N_KV_HEADS):
        ln = slice(h * HEAD_DIM, (h + 1) * HEAD_DIM)
        kh = rope_head(head_norm(z[:, K_OFF + h * HEAD_DIM:K_OFF + (h + 1) * HEAD_DIM], kn_ref[...]))
        vh = z[:, V_OFF + h * HEAD_DIM:V_OFF + (h + 1) * HEAD_DIM]
        k_ref[pl.ds(h, R, stride=N_KV_HEADS), :] = kh
        v_ref[pl.ds(h, R, stride=N_KV_HEADS), :] = vh
        kb_ref[:, ln] = kh.astype(BF16)
        vb_ref[:, ln] = vh.astype(BF16)
    for s in range(IDX_HEADS * IDX_DIM // LANES):
        ln = slice(s * LANES, (s + 1) * LANES)
        qi_ref[:, ln] = rope_idx(z[:, QI_OFF + s * LANES:QI_OFF + (s + 1) * LANES]).astype(BF16)
    kw = z[:, KW_OFF:]
    kw_ref[...] = jnp.where(lane < IDX_DIM, rope_idx(kw), kw * IDX_SCALE)


def _attn_project(x, pos_tab, r_tile, layer, n_layers, kv_stacked, norm_g, w_in, q_norm, k_norm):
    N, D = x.shape
    P = pos_tab.shape[0]
    assert N % r_tile == 0 and P % r_tile == 0
    w = jnp.pad(w_in, ((0, 0), (0, ATTN_COLS - w_in.shape[1]))).astype(BF16)
    cq, sq = _rope_tables(pos_tab, HEAD_DIM)
    ci, si = _rope_tables(pos_tab, IDX_DIM)
    n_tab = P // r_tile
    row_spec = lambda c: pl.BlockSpec((r_tile, c), lambda i: (i, 0))
    tab_spec = pl.BlockSpec((r_tile, LANES), lambda i: (i % n_tab, 0))
    stack_spec = pl.BlockSpec((None, r_tile * N_KV_HEADS, HEAD_DIM), lambda i: (layer, i, 0))
    stack_shape = jax.ShapeDtypeStruct((n_layers, N * N_KV_HEADS, HEAD_DIM), F32)
    row = lambda v: v.reshape(1, -1)
    kv_w = N_KV_HEADS * HEAD_DIM
    rows_out = lambda c, dt: (row_spec(c), jax.ShapeDtypeStruct((N, c), dt))
    outs = (rows_out(N_HEADS * HEAD_DIM, BF16), (stack_spec, stack_shape), (stack_spec, stack_shape),
            rows_out(kv_w, BF16), rows_out(kv_w, BF16), rows_out(IDX_HEADS * IDX_DIM, BF16), rows_out(LANES, F32))
    in_specs = [row_spec(D), _const_spec((1, D)), _const_spec(w.shape), _const_spec((1, HEAD_DIM)),
                _const_spec((1, HEAD_DIM)), tab_spec, tab_spec, tab_spec, tab_spec]
    if kv_stacked is None:
        kv_stacked = (jnp.zeros(stack_shape.shape, F32), jnp.zeros(stack_shape.shape, F32))
    stacked = tuple(kv_stacked)
    aliases = {len(in_specs) + n: 1 + n for n in range(len(stacked))}
    in_specs += [pl.BlockSpec(memory_space=pl.ANY)] * len(stacked)
    return pl.pallas_call(
        functools.partial(_proj_body, len(stacked)),
        grid=(N // r_tile,),
        in_specs=in_specs,
        out_specs=tuple(s for s, _ in outs),
        out_shape=tuple(s for _, s in outs),
        input_output_aliases=aliases,
        compiler_params=pltpu.CompilerParams(dimension_semantics=("arbitrary",),
                                             vmem_limit_bytes=VMEM_LIMIT),
        name="attn_project",
    )(x, row(norm_g), w, row(q_norm), row(k_norm), cq, sq, ci, si, *stacked)


BISECT_STEPS = 12


def _topk_bias(segs, k, tri_ref):
    kf = jnp.float32(k)
    R = segs[0][0].shape[0]

    def reduce_rows(fn, combine):
        tot = None
        for x, _ in segs:
            c = fn(x)
            tot = c if tot is None else combine(tot, c)
        return tot

    def count_ge(t):
        return reduce_rows(lambda x: jnp.sum(jnp.where(x >= t, 1.0, 0.0), axis=-1, keepdims=True), jnp.add)

    def narrow(t, lo, hi, c_hi):
        c = count_ge(t)
        up = c >= kf
        down = jnp.logical_not(up) & (t < hi)
        return jnp.where(up & (t > lo), t, lo), jnp.where(down, t, hi), jnp.where(down, c, c_hi)

    top = reduce_rows(lambda x: jnp.max(x, axis=-1, keepdims=True), jnp.maximum)
    bot = reduce_rows(lambda x: jnp.min(jnp.where(x > -jnp.inf, x, jnp.inf), axis=-1, keepdims=True), jnp.minimum)
    state = (jnp.full((R, 1), -jnp.inf, F32), jnp.full((R, 1), jnp.inf, F32), jnp.zeros((R, 1), F32))
    state = narrow(bot, *narrow(top, *state))
    lo, hi, c_hi = lax.fori_loop(0, BISECT_STEPS, lambda _, s: narrow(s[0] * 0.5 + s[1] * 0.5, *s), state)

    def close_cond(s):
        return s[4] > 0.0

    def close_step(s):
        lo, hi, c_hi, done, _ = s
        v = reduce_rows(lambda x: jnp.max(jnp.where(x < hi, x, -jnp.inf), axis=-1, keepdims=True), jnp.maximum)
        c = count_ge(v)
        active = done < 0.5
        last = active & (c >= kf)
        more = active & (c < kf)
        done = jnp.where(last, 1.0, done)
        return (jnp.where(last, v, lo), jnp.where(more, v, hi), jnp.where(more, c, c_hi), done,
                jnp.sum(1.0 - done))

    lo, hi, c_hi, _, _ = lax.while_loop(close_cond, close_step,
                                        (lo, hi, c_hi, jnp.zeros((R, 1), F32), jnp.float32(R)))
    need = kf - c_hi
    off = jnp.zeros_like(need)
    out = []
    for x, valid_fn in segs:
        n = x.shape[-1]
        w = min(n, LANES)
        pieces = []
        for c in range(n // w):
            xc = x[:, c * w:(c + 1) * w]
            above = jnp.where(xc >= hi, 1.0, 0.0)
            tie = jnp.where(xc >= lo, 1.0, 0.0) - above
            pref = jnp.dot(tie.astype(BF16), tri_ref[:w, :w], preferred_element_type=F32)
            take = above + tie * jnp.where(off + pref <= need, 1.0, 0.0)
            pieces.append(jnp.where(valid_fn(c * w, w), jnp.where(take > 0.5, 0.0, -jnp.inf), -jnp.inf))
            off = off + pref[:, w - 1:w]
        out.append(pieces[0] if len(pieces) == 1 else jnp.concatenate(pieces, axis=-1))
    return out


def _dsa_prompt_body(topk, tile0, q_ref, qi_ref, kwq_ref, k_ref, v_ref, kwk_ref, h_ref, wout_ref, tri_ref, out_ref):
    Tq = q_ref.shape[0]
    S = k_ref.shape[0]
    grp = N_HEADS // N_KV_HEADS
    qpos0 = (tile0 + pl.program_id(1)) * Tq

    def causal(c0, w):
        qpos = qpos0 + lax.broadcasted_iota(jnp.int32, (Tq, w), 0)
        return c0 + lax.broadcasted_iota(jnp.int32, (Tq, w), 1) <= qpos

    ki = kwk_ref[:, :IDX_DIM].astype(BF16)
    wq = kwq_ref[...]
    scores = None
    for h in range(IDX_HEADS):
        s = _dot_nt(qi_ref[:, h * IDX_DIM:(h + 1) * IDX_DIM], ki)
        t = jnp.maximum(s, 0.0) * wq[:, IDX_DIM + h:IDX_DIM + h + 1]
        scores = t if scores is None else scores + t
    scores = jnp.where(causal(0, S), scores, -jnp.inf)
    (bias,) = _topk_bias([(scores, causal)], topk, tri_ref)

    outs = []
    for g in range(N_KV_HEADS):
        ln = slice(g * HEAD_DIM, (g + 1) * HEAD_DIM)
        qs = jnp.concatenate([q_ref[:, (g * grp + hh) * HEAD_DIM:(g * grp + hh + 1) * HEAD_DIM] for hh in range(grp)],
                             axis=0)
        s = _dot_nt(qs, k_ref[:, ln])
        ps, ls = [], []
        for hh in range(grp):
            sh = s[hh * Tq:(hh + 1) * Tq] + bias
            p = jnp.exp(sh - jnp.max(sh, axis=-1, keepdims=True))
            ls.append(jnp.sum(p, axis=-1, keepdims=True))
            ps.append(p.astype(BF16))
        o = _dot(jnp.concatenate(ps, axis=0), v_ref[:, ln])
        outs += [o[hh * Tq:(hh + 1) * Tq] / ls[hh] for hh in range(grp)]
    out_ref[...] = h_ref[...] + _dot(jnp.concatenate(outs, axis=-1), wout_ref[...])


def _tri(n):
    i = jnp.arange(n)
    return (i[:, None] <= i[None, :]).astype(BF16)


def _dsa_prompt(h, q, k, v, qi, kw, w_out, q_tile):
    B, T, D = h.shape
    topk = min(TOPK_MAX, T // 4)
    wb, tri = w_out.astype(BF16), _tri(LANES)
    for tile in range(T // q_tile):
        S = (tile + 1) * q_tile
        qspec = lambda c, tile=tile: pl.BlockSpec((None, q_tile, c), lambda b, j: (b, tile + j, 0))
        kspec = lambda c, S=S: pl.BlockSpec((None, S, c), lambda b, j: (b, 0, 0))
        h = pl.pallas_call(
            functools.partial(_dsa_prompt_body, topk, tile),
            grid=(B, 1),
            in_specs=[qspec(q.shape[-1]), qspec(qi.shape[-1]), qspec(LANES), kspec(k.shape[-1]), kspec(v.shape[-1]),
                      kspec(LANES), qspec(D), _const_spec(wb.shape), _const_spec((LANES, LANES))],
            out_specs=qspec(D),
            out_shape=jax.ShapeDtypeStruct(h.shape, F32),
            input_output_aliases={6: 0},
            compiler_params=pltpu.CompilerParams(dimension_semantics=("arbitrary", "arbitrary"),
                                                 vmem_limit_bytes=VMEM_LIMIT),
            name="dsa_prompt",
        )(q, qi, kw, k, v, kw, h, wb, tri)
    return h


PAGES_PER_CHUNK = 16
Q_ROWS = SUBLANES


def _sample_index_body(layer, n_pages, pt_ref, qi_ref, wq_ref, kinew_ref, ckidx_ref, out_ref, kidx_buf, sems):
    b = pl.program_id(0)
    nb = pl.num_programs(0)
    past = n_pages * PAGE_SIZE
    n_tok = out_ref.shape[0]

    def kidx_copy(bb, p):
        return pltpu.make_async_copy(ckidx_ref.at[layer, pt_ref[bb, p]], kidx_buf.at[bb % 2, p], sems.at[bb % 2])

    def start_kidx(bb):
        for p in range(n_pages):
            kidx_copy(bb, p).start()

    @pl.when(b == 0)
    def _():
        start_kidx(b)

    for p in range(n_pages):
        kidx_copy(b, p).wait()

    @pl.when(b + 1 < nb)
    def _():
        start_kidx(b + 1)

    qi = qi_ref[...]
    wq = wq_ref[...]

    def head_sum(s):
        return jnp.sum((jnp.maximum(s, 0.0) * wq).reshape(n_tok, IDX_HEADS, s.shape[-1]), axis=1)

    for p in range(n_pages):
        out_ref[:, p * PAGE_SIZE:(p + 1) * PAGE_SIZE] = head_sum(_dot(qi, kidx_buf[b % 2, p]))
    i_new = head_sum(_dot_nt(qi, kinew_ref[:, :IDX_DIM]))
    tq = lax.broadcasted_iota(jnp.int32, i_new.shape, 0)
    out_ref[:, past:] = jnp.where(lax.broadcasted_iota(jnp.int32, i_new.shape, 1) <= tq, i_new, -jnp.inf)


def _sample_index_scores(qi, kw, cache_kidx, page_table, layer):
    B, n_tok, _ = qi.shape
    n_pages = page_table.shape[1]
    rows = n_tok * IDX_HEADS
    wq_rows = kw[:, :, IDX_DIM:IDX_DIM + IDX_HEADS].reshape(B, rows, 1)
    ki_new = jnp.pad(kw, ((0, 0), (0, LANES - n_tok), (0, 0)))
    cki = jnp.swapaxes(cache_kidx, 2, 3)
    bspec = lambda *s: pl.BlockSpec((None,) + s, lambda b, pt: (b,) + (0,) * len(s))
    grid_spec = pltpu.PrefetchScalarGridSpec(
        num_scalar_prefetch=1,
        grid=(B,),
        in_specs=[bspec(rows, IDX_DIM), bspec(rows, 1), bspec(LANES, LANES), pl.BlockSpec(memory_space=pl.ANY)],
        out_specs=bspec(n_tok, n_pages * PAGE_SIZE + LANES),
        scratch_shapes=[pltpu.VMEM((2, n_pages, IDX_DIM, PAGE_SIZE), F32), pltpu.SemaphoreType.DMA((2,))],
    )
    return pl.pallas_call(
        functools.partial(_sample_index_body, layer, n_pages),
        grid_spec=grid_spec,
        out_shape=jax.ShapeDtypeStruct((B, n_tok, n_pages * PAGE_SIZE + LANES), F32),
        compiler_params=pltpu.CompilerParams(dimension_semantics=("arbitrary",), vmem_limit_bytes=VMEM_LIMIT),
        name="sample_index_scores",
    )(page_table, qi.reshape(B, rows, IDX_DIM), wq_rows, ki_new, cki)


def _sample_topk_body(topk, n_tok, past, s_ref, tri_ref, out_ref):
    R = s_ref.shape[0]
    tq = lax.broadcasted_iota(jnp.int32, (R, LANES), 0) % n_tok

    def valid_past(c0, w):
        return lax.broadcasted_iota(jnp.int32, (R, w), 1) >= 0

    def valid_new(c0, w):
        return c0 + lax.broadcasted_iota(jnp.int32, (R, w), 1) <= tq[:, :w]

    bias_past, bias_new = _topk_bias([(s_ref[:, :past], valid_past), (s_ref[:, past:], valid_new)], topk, tri_ref)
    out_ref[:, :past] = bias_past
    out_ref[:, past:] = bias_new


def _sample_topk_bias(scores, n_tok, topk):
    R, n = scores.shape
    return pl.pallas_call(
        functools.partial(_sample_topk_body, topk, n_tok, n - LANES),
        grid=(1,),
        in_specs=[_const_spec((R, n)), _const_spec((LANES, LANES))],
        out_specs=pl.BlockSpec((R, n), lambda i: (0, 0)),
        out_shape=jax.ShapeDtypeStruct((R, n), F32),
        compiler_params=pltpu.CompilerParams(dimension_semantics=("arbitrary",), vmem_limit_bytes=VMEM_LIMIT),
        name="sample_topk_bias",
    )(scores, _tri(LANES))


def _sample_attend_body(layer, n_pages, pt_ref, q8_ref, knew_ref, vnew_ref, bias_ref, h_ref, wout_ref,
                        ck_ref, cv_ref, out_ref, kv_buf, sc_ref, sems):
    b = pl.program_id(0)
    nb = pl.num_programs(0)
    n_chunks = n_pages // PAGES_PER_CHUNK
    n_loads = 2 * n_chunks
    past = n_pages * PAGE_SIZE
    page_rows = PAGE_SIZE * N_KV_HEADS
    ch_keys = PAGES_PER_CHUNK * PAGE_SIZE
    n_tok = h_ref.shape[0]

    def kv_copy(bb, i, p):
        src = ck_ref if i < n_chunks else cv_ref
        page = pt_ref[bb, (i % n_chunks) * PAGES_PER_CHUNK + p]
        return pltpu.make_async_copy(src.at[layer, page], kv_buf.at[i % 2, pl.ds(p * page_rows, page_rows)],
                                     sems.at[i % 2])

    def start_load(bb, i):
        for p in range(PAGES_PER_CHUNK):
            kv_copy(bb, i, p).start()

    def wait_load(i):
        for p in range(PAGES_PER_CHUNK):
            kv_copy(b, i, p).wait()

    def start_after(i):
        if i + 2 < n_loads:
            start_load(b, i + 2)
        else:
            @pl.when(b + 1 < nb)
            def _():
                start_load(b + 1, i + 2 - n_loads)

    def head_rows(slot, g):
        return kv_buf[slot, pl.ds(g, ch_keys, stride=N_KV_HEADS), :]

    @pl.when(b == 0)
    def _():
        start_load(b, 0)
        start_load(b, 1)

    bias_new = bias_ref[:, past:past + Q_ROWS]

    for c in range(n_chunks):
        wait_load(c)
        ks = slice(c * ch_keys, (c + 1) * ch_keys)
        for g in range(N_KV_HEADS):
            sc_ref[g, :, ks] = _dot_nt(q8_ref[g], head_rows(c % 2, g)) + bias_ref[:, ks]
        start_after(c)

    acc, denom = [], []
    for g in range(N_KV_HEADS):
        ln = slice(g * HEAD_DIM, (g + 1) * HEAD_DIM)
        s_new = _dot_nt(q8_ref[g], knew_ref[:, ln]) + bias_new
        s_past = sc_ref[g]
        m = jnp.maximum(jnp.max(s_past, axis=-1, keepdims=True), jnp.max(s_new, axis=-1, keepdims=True))
        p_past = jnp.exp(s_past - m)
        p_new = jnp.exp(s_new - m)
        sc_ref[g] = p_past
        denom.append(jnp.sum(p_past, axis=-1, keepdims=True) + jnp.sum(p_new, axis=-1, keepdims=True))
        acc.append(_dot(p_new, vnew_ref[:, ln]))

    for c in range(n_chunks):
        i = n_chunks + c
        wait_load(i)
        ks = slice(c * ch_keys, (c + 1) * ch_keys)
        for g in range(N_KV_HEADS):
            acc[g] = acc[g] + _dot(sc_ref[g, :, ks], head_rows(i % 2, g))
        start_after(i)

    heads = []
    for g in range(N_KV_HEADS):
        og = acc[g] / denom[g]
        for hh in range(N_HEADS // N_KV_HEADS):
            heads.append(og[hh * n_tok:(hh + 1) * n_tok])
    o = jnp.concatenate(heads, axis=-1)
    out_ref[...] = h_ref[...] + _dot(o, wout_ref[...])


def _dsa_sample(h, q, k, v, qi, kw, cache_k, cache_v, cache_kidx, page_table, layer, w_out):
    B, n_tok, D = h.shape
    grp = N_HEADS // N_KV_HEADS
    assert grp * n_tok == Q_ROWS
    n_pages = page_table.shape[1]
    assert n_pages % PAGES_PER_CHUNK == 0
    past = n_pages * PAGE_SIZE
    topk = min(TOPK_MAX, (past + n_tok) // 4)
    scores = _sample_index_scores(qi, kw, cache_kidx, page_table, layer)
    bias = _sample_topk_bias(scores.reshape(B * n_tok, past + LANES), n_tok, topk).reshape(scores.shape)
    bias = jnp.concatenate([bias] * grp, axis=1)
    q8 = q.reshape(B, n_tok, N_KV_HEADS, grp, HEAD_DIM).transpose(0, 2, 3, 1, 4).reshape(B, N_KV_HEADS, Q_ROWS, HEAD_DIM)
    pad8 = lambda a: jnp.pad(a, ((0, 0), (0, Q_ROWS - n_tok), (0, 0)))
    ck = cache_k.reshape(cache_k.shape[0], cache_k.shape[1], PAGE_SIZE * N_KV_HEADS, HEAD_DIM)
    cv = cache_v.reshape(ck.shape)
    bspec = lambda *s: pl.BlockSpec((None,) + s, lambda b, pt: (b,) + (0,) * len(s))
    any_spec = pl.BlockSpec(memory_space=pl.ANY)
    grid_spec = pltpu.PrefetchScalarGridSpec(
        num_scalar_prefetch=1,
        grid=(B,),
        in_specs=[bspec(N_KV_HEADS, Q_ROWS, HEAD_DIM), bspec(Q_ROWS, k.shape[-1]), bspec(Q_ROWS, v.shape[-1]),
                  bspec(Q_ROWS, past + LANES), bspec(n_tok, D),
                  pl.BlockSpec(w_out.shape, lambda b, pt: (0, 0), pipeline_mode=pl.Buffered(1)),
                  any_spec, any_spec],
        out_specs=bspec(n_tok, D),
        scratch_shapes=[pltpu.VMEM((2, PAGES_PER_CHUNK * PAGE_SIZE * N_KV_HEADS, HEAD_DIM), F32),
                        pltpu.VMEM((N_KV_HEADS, Q_ROWS, past), F32),
                        pltpu.SemaphoreType.DMA((2,))],
    )
    return pl.pallas_call(
        functools.partial(_sample_attend_body, layer, n_pages),
        grid_spec=grid_spec,
        out_shape=jax.ShapeDtypeStruct(h.shape, F32),
        compiler_params=pltpu.CompilerParams(dimension_semantics=("arbitrary",),
                                             vmem_limit_bytes=VMEM_LIMIT),
        name="sample_attend",
    )(page_table, q8, pad8(k), pad8(v), bias, h, w_out.astype(BF16), ck, cv)


PROMPT_T_TILE = 64
PROJ_ROW_TILE = 512
PROMPT_Q_TILE = 256


def kernel(x_prompt, x_sample, cache_k, cache_v, cache_kidx, state_rec_conv, state_rec_h, state_pool, state_ffn_conv, page_table, p_prompt, p_sample, norm_mix, norm_ffn, norm_ple, w_in_rec, conv_rec_w, conv_rec_b, w_rgate, b_rgate, w_igate, b_igate, lru_lambda, w_pool, pool_scale, w_out_rec, w_in_attn, q_norm, k_norm, w_out_attn, w_up, conv_ff_w, conv_ff_b, w_down, w_ple, w_ple_gate):
    B, T, D = x_prompt.shape
    Bs, Ts, _ = x_sample.shape
    depth = norm_mix.shape[0]
    C = conv_rec_w.shape[-1]
    FF = conv_ff_w.shape[-1]
    past = page_table.shape[1] * PAGE_SIZE
    hp = x_prompt
    hs = jnp.swapaxes(x_sample, 0, 1)
    zeros = lambda *s: jnp.zeros(s, F32)
    n_attn = depth // 2
    kv_p = kv_s = None
    ffn_w = tuple(w.astype(BF16) for w in (w_up, w_down, w_ple, w_ple_gate))
    p_sample_t = jnp.swapaxes(p_sample, 1, 2)
    outs = {n: [] for n in ("rc_p", "rc_s", "rh_p", "rh_s", "pl_p", "pl_s", "ki_p", "ki_s", "fc_p", "fc_s")}
    for i in range(depth):
        j = i // 2
        if i % 2 == 0:
            wts = (norm_mix[i], w_in_rec[j], conv_rec_w[j], conv_rec_b[j], w_rgate[j], b_rgate[j],
                   w_igate[j], b_igate[j], lru_lambda[j], w_pool[j], pool_scale[j], w_out_rec[j])
            hp, c, hh, pb = _rec_layer(hp, 1, PROMPT_T_TILE, 0, zeros(B, HIST_CONV, C), zeros(B, 1, C),
                                       zeros(B, HIST_POOL, C), *wts)
            outs["rc_p"].append(_unpad_hist(c, CONV_REC - 1, 1))
            outs["rh_p"].append(hh[:, 0])
            outs["pl_p"].append(_unpad_hist(pb, POOL_BUF, 1))
            hs, c, hh, pb = _rec_layer(hs, 0, Ts, past, _pad_hist(state_rec_conv[j], HIST_CONV, 0),
                                       state_rec_h[j][None], _pad_hist(state_pool[j], HIST_POOL, 0), *wts)
            outs["rc_s"].append(_unpad_hist(c, CONV_REC - 1, 0))
            outs["rh_s"].append(hh[0])
            outs["pl_s"].append(_unpad_hist(pb, POOL_BUF, 0))
        else:
            pw = (norm_mix[i], w_in_attn[j], q_norm[j], k_norm[j])
            q, k, v, kb, vb, qi, kw = _attn_project(hp.reshape(B * T, D), jnp.arange(T), PROJ_ROW_TILE, j, n_attn,
                                                    kv_p, *pw)
            kv_p = (k, v)
            r3 = lambda a: a.reshape(B, T, -1)
            hp = _dsa_prompt(hp, r3(q), r3(kb), r3(vb), r3(qi), r3(kw), w_out_attn[j], PROMPT_Q_TILE)
            outs["ki_p"].append(r3(kw)[..., :IDX_DIM])
            pos_s = past + jnp.arange(Ts * Bs) // Bs
            q, k, v, kb, vb, qi, kw = _attn_project(hs.reshape(Ts * Bs, D), pos_s, Ts * Bs, j, n_attn, kv_s, *pw)
            kv_s = (k, v)
            bm = lambda a: jnp.swapaxes(a.reshape(Ts, Bs, -1), 0, 1)
            kw = bm(kw)
            hs = jnp.swapaxes(_dsa_sample(bm(hs), bm(q), bm(kb), bm(vb), bm(qi), kw, cache_k, cache_v, cache_kidx,
                                          page_table, j, w_out_attn[j]), 0, 1)
            outs["ki_s"].append(kw[..., :IDX_DIM])
        fw = (norm_ffn[i], ffn_w[0], conv_ff_w[i], conv_ff_b[i], ffn_w[1], norm_ple[i], ffn_w[2], ffn_w[3])
        hp, fb = _ffn_layer(hp, p_prompt, i, 1, PROMPT_T_TILE, zeros(B, HIST_FFN, FF), *fw)
        outs["fc_p"].append(_unpad_hist(fb, CONV_FF - 1, 1))
        hs, fb = _ffn_layer(hs, p_sample_t, i, 0, Ts, _pad_hist(state_ffn_conv[i], HIST_FFN, 0), *fw)
        outs["fc_s"].append(_unpad_hist(fb, CONV_FF - 1, 0))
    st = lambda n: jnp.stack(outs[n])
    kv_prompt = lambda a: a.reshape(n_attn, B, T, N_KV_HEADS, HEAD_DIM)
    kv_sample = lambda a: jnp.swapaxes(a.reshape(n_attn, Ts, Bs, N_KV_HEADS, HEAD_DIM), 1, 2)
    return (hp, jnp.swapaxes(hs, 0, 1), st("rc_p"), st("rc_s"), st("rh_p"), st("rh_s"), st("pl_p"), st("pl_s"),
            kv_prompt(kv_p[0]), kv_sample(kv_s[0]), kv_prompt(kv_p[1]), kv_sample(kv_s[1]),
            st("ki_p"), st("ki_s"), st("fc_p"), st("fc_s"))
```

```python
import functools
import math

import jax
import jax.numpy as jnp
from jax import lax
from jax.experimental import pallas as pl
from jax.experimental.pallas import tpu as pltpu

F32 = jnp.float32
BF16 = jnp.bfloat16
EPS = 1e-6
LRU_C = 8.0
POOL_WINDOWS = (2, 4, 8, 16)
REC_HEADS = 8
CONV_REC = 4
CONV_FF = 3
POOL_BUF = max(POOL_WINDOWS) - 1
N_HEADS = 8
HEAD_DIM = 128
N_KV_HEADS = 4
IDX_HEADS = 8
IDX_DIM = 64
TOPK_MAX = 256
ROPE_THETA = 10000.0
PAGE_SIZE = 128

LANES = 128
SUBLANES = 8
VMEM_LIMIT = 56 * 1024 * 1024

HIST_CONV = SUBLANES
HIST_POOL = 2 * SUBLANES
HIST_FFN = SUBLANES


def _rms(x, g):
    return x * lax.rsqrt(jnp.mean(x * x, axis=-1, keepdims=True) + EPS) * g


def _dot(a, b):
    return jnp.dot(a.astype(BF16), b.astype(BF16), preferred_element_type=F32)


def _dot_nt(a, b):
    return lax.dot_general(a.astype(BF16), b.astype(BF16), (((1,), (1,)), ((), ())),
                           preferred_element_type=F32)


def _softplus(x):
    return jnp.maximum(x, 0.0) + jnp.log1p(jnp.exp(-jnp.abs(x)))


def _expm1(x):
    u = jnp.exp(x)
    near = (u - 1.0) * x / jnp.log(u)
    return jnp.where(u == 1.0, x, jnp.where(jnp.abs(x) > 0.5, u - 1.0, near))


def _tix(ta, start, size):
    t = slice(start, start + size) if isinstance(start, int) else pl.ds(start, size)
    return (slice(None), t, slice(None)) if ta == 1 else (t, slice(None), slice(None))


def _const_spec(shape):
    nd = len(shape)
    return pl.BlockSpec(shape, lambda *_: (0,) * nd, pipeline_mode=pl.Buffered(1))


def _tile_spec(shape, ta):
    if ta == 1:
        return pl.BlockSpec(shape, lambda i: (0, i, 0))
    return pl.BlockSpec(shape, lambda i: (i, 0, 0))


def _rec_body(ta, pos0, h_ref, g_ref, win_ref, cw_ref, cb_ref, wr_ref, br_ref, wi_ref, bi_ref,
              lam_ref, wp_ref, ps_ref, wout_ref, conv0_ref, h0_ref, pool0_ref,
              out_ref, conv_out_ref, hlast_ref, pool_out_ref,
              xa_ext, xb_ext, a_s, u_s, hs_s, hc_s):
    step = pl.program_id(0)
    A, S, D = h_ref.shape
    R = A * S
    n_t = S if ta == 1 else A
    C = xa_ext.shape[-1]

    @pl.when(step == 0)
    def _():
        xa_ext[_tix(ta, 0, HIST_CONV)] = conv0_ref[...]
        xb_ext[_tix(ta, 0, HIST_POOL)] = pool0_ref[...]
        hc_s[...] = h0_ref[...]

    x = h_ref[...].reshape(R, D)
    z = _dot(_rms(x, g_ref[...]), win_ref[...])
    ga = z[:, C:2 * C]
    xa_ext[_tix(ta, HIST_CONV, n_t)] = z[:, :C].reshape(A, S, C)
    xb_ext[_tix(ta, HIST_POOL, n_t)] = z[:, 2 * C:].reshape(A, S, C)

    xc = cb_ref[...][None]
    for k in range(CONV_REC):
        xc = xc + xa_ext[_tix(ta, HIST_CONV - (CONV_REC - 1) + k, n_t)] * cw_ref[k:k + 1, :][None]
    xc = xc.reshape(R, C)

    r = jax.nn.sigmoid(_dot(xc, wr_ref[...]) + br_ref[...])
    gi = jax.nn.sigmoid(_dot(xc, wi_ref[...]) + bi_ref[...])
    log_a = (-LRU_C * r) * _softplus(-lam_ref[...])
    a_s[...] = jnp.exp(log_a).reshape(A, S, C)
    u_s[...] = (jnp.sqrt(-_expm1(2.0 * log_a)) * gi * xc).reshape(A, S, C)

    def scan_step(t, h):
        ix = _tix(ta, t, 1)
        h = a_s[ix] * h + u_s[ix]
        hs_s[ix] = h
        return h

    if n_t <= SUBLANES:
        h = hc_s[...]
        for t in range(n_t):
            h = scan_step(t, h)
    else:
        h = lax.fori_loop(0, n_t, scan_step, hc_s[...], unroll=SUBLANES)
    hc_s[...] = h
    hlast_ref[...] = h
    ya = hs_s[...].reshape(R, C) * jax.nn.gelu(ga)

    G = C // len(POOL_WINDOWS)
    pos = pos0 + step * n_t + lax.broadcasted_iota(jnp.int32, (A, S, G), ta)
    posf = pos.astype(F32) + 1.0
    ds = []
    for gidx, w in enumerate(POOL_WINDOWS):
        ln = slice(gidx * G, (gidx + 1) * G)
        cur = xb_ext[_tix(ta, HIST_POOL, n_t)[:2] + (ln,)]
        acc = cur
        for j in range(1, w):
            acc = acc + xb_ext[_tix(ta, HIST_POOL - j, n_t)[:2] + (ln,)]
        ds.append(acc / jnp.minimum(jnp.float32(w), posf) - cur)
    d = jnp.concatenate(ds, axis=-1).reshape(R, C)
    yb = _dot(d, wp_ref[...]) * ps_ref[...]

    y = _dot(ya, wout_ref[:C, :]) + _dot(yb, wout_ref[C:, :])
    out_ref[...] = (x + y).reshape(A, S, D)

    new_conv = xa_ext[_tix(ta, n_t, HIST_CONV)]
    xa_ext[_tix(ta, 0, HIST_CONV)] = new_conv
    conv_out_ref[...] = new_conv
    new_pool = xb_ext[_tix(ta, n_t, HIST_POOL)]
    xb_ext[_tix(ta, 0, HIST_POOL)] = new_pool
    pool_out_ref[...] = new_pool


def _block_diag(w):
    H, a, b = w.shape
    eye = jnp.eye(H, dtype=w.dtype)
    return (eye[:, None, :, None] * w[:, :, None, :]).reshape(H * a, H * b)


def _rec_layer(h, ta, t_tile, pos0, conv0, h0, pool0, norm_g, w_in, conv_w, conv_b,
               w_r, b_r, w_i, b_i, lam, w_pool, pool_scale, w_out):
    D = h.shape[-1]
    C = conv_w.shape[-1]
    n_time = h.shape[ta]
    n_batch = h.shape[1 - ta]
    assert n_time % t_tile == 0
    if ta == 1:
        blk = (n_batch, t_tile, D)
        ext = lambda hist: (n_batch, hist + t_tile, C)
        hist_shape = lambda hist: (n_batch, hist, C)
        row_shape = (n_batch, 1, C)
        tile_c = (n_batch, t_tile, C)
    else:
        blk = (t_tile, n_batch, D)
        ext = lambda hist: (hist + t_tile, n_batch, C)
        hist_shape = lambda hist: (hist, n_batch, C)
        row_shape = (1, n_batch, C)
        tile_c = (t_tile, n_batch, C)
    row = lambda v: v.reshape(1, -1)
    args = (h, row(norm_g), w_in.astype(BF16), conv_w, row(conv_b),
            _block_diag(w_r).astype(BF16), row(b_r), _block_diag(w_i).astype(BF16), row(b_i),
            row(lam), _block_diag(w_pool).astype(BF16), row(pool_scale), w_out.astype(BF16),
            conv0, h0, pool0)
    in_specs = [_tile_spec(blk, ta)] + [_const_spec(a.shape) for a in args[1:]]
    out_shape = (jax.ShapeDtypeStruct(h.shape, F32),
                 jax.ShapeDtypeStruct(hist_shape(HIST_CONV), F32),
                 jax.ShapeDtypeStruct(row_shape, F32),
                 jax.ShapeDtypeStruct(hist_shape(HIST_POOL), F32))
    out_specs = (_tile_spec(blk, ta),
                 pl.BlockSpec(hist_shape(HIST_CONV), lambda i: (0, 0, 0)),
                 pl.BlockSpec(row_shape, lambda i: (0, 0, 0)),
                 pl.BlockSpec(hist_shape(HIST_POOL), lambda i: (0, 0, 0)))
    scratch = [pltpu.VMEM(ext(HIST_CONV), F32), pltpu.VMEM(ext(HIST_POOL), F32),
               pltpu.VMEM(tile_c, F32), pltpu.VMEM(tile_c, F32), pltpu.VMEM(tile_c, F32),
               pltpu.VMEM(row_shape, F32)]
    return pl.pallas_call(
        functools.partial(_rec_body, ta, pos0),
        grid=(n_time // t_tile,),
        in_specs=in_specs, out_specs=out_specs, out_shape=out_shape, scratch_shapes=scratch,
        compiler_params=pltpu.CompilerParams(dimension_semantics=("arbitrary",),
                                             vmem_limit_bytes=VMEM_LIMIT),
        name="rec_pool_mixer",
    )(*args)


def _ffn_body(ta, n_chunks, h_ref, p_ref, gf_ref, wup_ref, cw_ref, cb_ref, wdn_ref, gp_ref,
              wple_ref, wpg_ref, buf0_ref, out_ref, buf_out_ref, g_ext):
    step = pl.program_id(0)
    A, S, D = h_ref.shape
    R = A * S
    n_t = S if ta == 1 else A
    FF = g_ext.shape[-1]
    ch = FF // n_chunks

    @pl.when(step == 0)
    def _():
        g_ext[_tix(ta, 0, HIST_FFN)] = buf0_ref[...]

    x = h_ref[...].reshape(R, D)
    xn = _rms(x, gf_ref[...]).astype(BF16)
    acc = jnp.zeros((R, D), F32)
    for c in range(n_chunks):
        ln = slice(c * ch, (c + 1) * ch)
        g = jnp.dot(xn, wup_ref[:, ln], preferred_element_type=F32)
        u = jnp.dot(xn, wup_ref[:, FF + c * ch:FF + (c + 1) * ch], preferred_element_type=F32)
        g_ext[_tix(ta, HIST_FFN, n_t)[:2] + (ln,)] = g.reshape(A, S, ch)
        gc = cb_ref[:, ln][None]
        for k in range(CONV_FF):
            gc = gc + (g_ext[_tix(ta, HIST_FFN - (CONV_FF - 1) + k, n_t)[:2] + (ln,)]
                       * cw_ref[k:k + 1, ln][None])
        act = jax.nn.gelu(gc).reshape(R, ch) * u
        acc = acc + _dot(act, wdn_ref[ln, :])
    h1 = x + acc
    gate = jax.nn.sigmoid(_dot(_rms(h1, gp_ref[...]), wpg_ref[...]))
    pp = _dot(p_ref[...].reshape(R, p_ref.shape[-1]), wple_ref[...])
    out_ref[...] = (h1 + pp * gate).reshape(A, S, D)

    new_buf = g_ext[_tix(ta, n_t, HIST_FFN)]
    g_ext[_tix(ta, 0, HIST_FFN)] = new_buf
    buf_out_ref[...] = new_buf


def _ffn_layer(h, p, layer, ta, t_tile, buf0, norm_f, w_up, conv_w, conv_b, w_down, norm_p, w_ple, w_pg,
               n_chunks=1):
    D = h.shape[-1]
    FF = conv_w.shape[-1]
    n_time = h.shape[ta]
    n_batch = h.shape[1 - ta]
    assert n_time % t_tile == 0 and FF % (n_chunks * LANES) == 0
    if ta == 1:
        blk = lambda c: (n_batch, t_tile, c)
        p_spec = pl.BlockSpec((None,) + blk(p.shape[-1]), lambda i: (layer, 0, i, 0))
        ext = (n_batch, HIST_FFN + t_tile, FF)
        hist = (n_batch, HIST_FFN, FF)
    else:
        blk = lambda c: (t_tile, n_batch, c)
        p_spec = pl.BlockSpec((None,) + blk(p.shape[-1]), lambda i: (layer, i, 0, 0))
        ext = (HIST_FFN + t_tile, n_batch, FF)
        hist = (HIST_FFN, n_batch, FF)
    row = lambda v: v.reshape(1, -1)
    w_spec = lambda w: pl.BlockSpec((None,) + w.shape[1:], lambda i: (layer, 0, 0), pipeline_mode=pl.Buffered(1))
    args = (h, p, row(norm_f), w_up, conv_w, row(conv_b), w_down, row(norm_p), w_ple, w_pg, buf0)
    in_specs = [_tile_spec(blk(D), ta), p_spec, _const_spec((1, D)), w_spec(w_up), _const_spec(conv_w.shape),
                _const_spec((1, FF)), w_spec(w_down), _const_spec((1, D)), w_spec(w_ple), w_spec(w_pg),
                _const_spec(buf0.shape)]
    out_shape = (jax.ShapeDtypeStruct(h.shape, F32), jax.ShapeDtypeStruct(hist, F32))
    out_specs = (_tile_spec(blk(D), ta), pl.BlockSpec(hist, lambda i: (0, 0, 0)))
    return pl.pallas_call(
        functools.partial(_ffn_body, ta, n_chunks),
        grid=(n_time // t_tile,),
        in_specs=in_specs, out_specs=out_specs, out_shape=out_shape,
        scratch_shapes=[pltpu.VMEM(ext, F32)],
        compiler_params=pltpu.CompilerParams(dimension_semantics=("arbitrary",),
                                             vmem_limit_bytes=VMEM_LIMIT),
        name="conv_ffn_ple",
    )(*args)


def _pad_hist(state, hist, ta):
    n = state.shape[1]
    if ta == 1:
        return jnp.pad(state, ((0, 0), (hist - n, 0), (0, 0)))
    return jnp.pad(jnp.swapaxes(state, 0, 1), ((hist - n, 0), (0, 0), (0, 0)))


def _unpad_hist(block, n, ta):
    if ta == 1:
        return block[:, block.shape[1] - n:, :]
    return jnp.swapaxes(block[block.shape[0] - n:], 0, 1)


Q_OFF = 0
K_OFF = N_HEADS * HEAD_DIM
V_OFF = K_OFF + N_KV_HEADS * HEAD_DIM
QI_OFF = V_OFF + N_KV_HEADS * HEAD_DIM
KW_OFF = QI_OFF + IDX_HEADS * IDX_DIM
ATTN_COLS = KW_OFF + LANES
IDX_SCALE = (IDX_DIM ** -0.5) * (IDX_HEADS ** -0.5)
ATTN_SCALE = HEAD_DIM ** -0.5


def _rope_tables(pos, dim):
    half = dim // 2
    inv = jnp.power(ROPE_THETA, -jnp.arange(half, dtype=F32) / half)
    ang = pos.astype(F32)[:, None] * inv[None, :]
    cos, sin = jnp.cos(ang), jnp.sin(ang)
    reps = LANES // dim
    return (jnp.tile(jnp.concatenate([cos, cos], axis=-1), (1, reps)),
            jnp.tile(jnp.concatenate([-sin, sin], axis=-1), (1, reps)))


def _proj_body(n_stacked, x_ref, g_ref, w_ref, qn_ref, kn_ref, cq_ref, sq_ref, ci_ref, si_ref, *refs):
    q_ref, k_ref, v_ref, kb_ref, vb_ref, qi_ref, kw_ref = refs[n_stacked:]
    R = x_ref.shape[0]
    z = _dot(_rms(x_ref[...], g_ref[...]), w_ref[...])
    cq, sq, ci, si = cq_ref[...], sq_ref[...], ci_ref[...], si_ref[...]
    lane = lax.broadcasted_iota(jnp.int32, cq.shape, 1)
    low_half = (lane % IDX_DIM) < (IDX_DIM // 2)

    def rope_head(t):
        return t * cq + pltpu.roll(t, HEAD_DIM // 2, 1) * sq

    def rope_idx(t):
        partner = jnp.where(low_half, pltpu.roll(t, LANES - IDX_DIM // 2, 1), pltpu.roll(t, IDX_DIM // 2, 1))
        return t * ci + partner * si

    def head_norm(t, g):
        return t * lax.rsqrt(jnp.mean(t * t, axis=-1, keepdims=True) + EPS) * g

    for h in range(N_HEADS):
        ln = slice(h * HEAD_DIM, (h + 1) * HEAD_DIM)
        q_ref[:, ln] = (rope_head(head_norm(z[:, ln], qn_ref[...])) * ATTN_SCALE).astype(BF16)
    for h in range(N_KV_HEADS):
        ln = slice(h * HEAD_DIM, (h + 1) * HEAD_DIM)
        kh = rope_head(head_norm(z[:, K_OFF + h * HEAD_DIM:K_OFF + (h + 1) * HEAD_DIM], kn_ref[...]))
        vh = z[:, V_OFF + h * HEAD_DIM:V_OFF + (h + 1) * HEAD_DIM]
        k_ref[pl.ds(h, R, stride=N_KV_HEADS), :] = kh
        v_ref[pl.ds(h, R, stride=N_KV_HEADS), :] = vh
        kb_ref[:, ln] = kh.astype(BF16)
        vb_ref[:, ln] = vh.astype(BF16)
    for s in range(IDX_HEADS * IDX_DIM // LANES):
        ln = slice(s * LANES, (s + 1) * LANES)
        qi_ref[:, ln] = rope_idx(z[:, QI_OFF + s * LANES:QI_OFF + (s + 1) * LANES]).astype(BF16)
    kw = z[:, KW_OFF:]
    kw_ref[...] = jnp.where(lane < IDX_DIM, rope_idx(kw), kw * IDX_SCALE)


def _attn_project(x, pos_tab, r_tile, layer, n_layers, kv_stacked, norm_g, w_in, q_norm, k_norm):
    N, D = x.shape
    P = pos_tab.shape[0]
    assert N % r_tile == 0 and P % r_tile == 0
    w = jnp.pad(w_in, ((0, 0), (0, ATTN_COLS - w_in.shape[1]))).astype(BF16)
    cq, sq = _rope_tables(pos_tab, HEAD_DIM)
    ci, si = _rope_tables(pos_tab, IDX_DIM)
    n_tab = P // r_tile
    row_spec = lambda c: pl.BlockSpec((r_tile, c), lambda i: (i, 0))
    tab_spec = pl.BlockSpec((r_tile, LANES), lambda i: (i % n_tab, 0))
    stack_spec = pl.BlockSpec((None, r_tile * N_KV_HEADS, HEAD_DIM), lambda i: (layer, i, 0))
    stack_shape = jax.ShapeDtypeStruct((n_layers, N * N_KV_HEADS, HEAD_DIM), F32)
    row = lambda v: v.reshape(1, -1)
    kv_w = N_KV_HEADS * HEAD_DIM
    rows_out = lambda c, dt: (row_spec(c), jax.ShapeDtypeStruct((N, c), dt))
    outs = (rows_out(N_HEADS * HEAD_DIM, BF16), (stack_spec, stack_shape), (stack_spec, stack_shape),
            rows_out(kv_w, BF16), rows_out(kv_w, BF16), rows_out(IDX_HEADS * IDX_DIM, BF16), rows_out(LANES, F32))
    in_specs = [row_spec(D), _const_spec((1, D)), _const_spec(w.shape), _const_spec((1, HEAD_DIM)),
                _const_spec((1, HEAD_DIM)), tab_spec, tab_spec, tab_spec, tab_spec]
    if kv_stacked is None:
        kv_stacked = (jnp.zeros(stack_shape.shape, F32), jnp.zeros(stack_shape.shape, F32))
    stacked = tuple(kv_stacked)
    aliases = {len(in_specs) + n: 1 + n for n in range(len(stacked))}
    in_specs += [pl.BlockSpec(memory_space=pl.ANY)] * len(stacked)
    return pl.pallas_call(
        functools.partial(_proj_body, len(stacked)),
        grid=(N // r_tile,),
        in_specs=in_specs,
        out_specs=tuple(s for s, _ in outs),
        out_shape=tuple(s for _, s in outs),
        input_output_aliases=aliases,
        compiler_params=pltpu.CompilerParams(dimension_semantics=("arbitrary",),
                                             vmem_limit_bytes=VMEM_LIMIT),
        name="attn_project",
    )(x, row(norm_g), w, row(q_norm), row(k_norm), cq, sq, ci, si, *stacked)


BISECT_STEPS = 12


def _topk_bias(segs, k, tri_ref):
    kf = jnp.float32(k)
    R = segs[0][0].shape[0]

    def reduce_rows(fn, combine):
        tot = None
        for x, _ in segs:
            c = fn(x)
            tot = c if tot is None else combine(tot, c)
        return tot

    def count_ge(t):
        return reduce_rows(lambda x: jnp.sum(jnp.where(x >= t, 1.0, 0.0), axis=-1, keepdims=True), jnp.add)

    def narrow(t, lo, c_lo, hi, c_hi):
        c = count_ge(t)
        up = (c >= kf) & (t > lo)
        down = (c < kf) & (t < hi)
        return jnp.where(up, t, lo), jnp.where(up, c, c_lo), jnp.where(down, t, hi), jnp.where(down, c, c_hi)

    top = reduce_rows(lambda x: jnp.max(x, axis=-1, keepdims=True), jnp.maximum)
    bot = reduce_rows(lambda x: jnp.min(jnp.where(x > -jnp.inf, x, jnp.inf), axis=-1, keepdims=True), jnp.minimum)
    col = lambda v: jnp.full((R, 1), v, F32)
    state = narrow(bot, *narrow(top, col(-jnp.inf), col(jnp.inf), col(jnp.inf), col(0.0)))
    state = lax.fori_loop(0, BISECT_STEPS, lambda _, s: narrow(s[0] * 0.5 + s[2] * 0.5, *s), state)

    def close_cond(s):
        return s[5] > 0.0

    def close_step(s):
        lo, c_lo, hi, c_hi, done, _ = s
        v = reduce_rows(lambda x: jnp.max(jnp.where(x < hi, x, -jnp.inf), axis=-1, keepdims=True), jnp.maximum)
        c = count_ge(v)
        active = done < 0.5
        last = active & (c >= kf)
        more = active & (c < kf)
        done = jnp.where(last, 1.0, done)
        return (jnp.where(last, v, lo), jnp.where(last, c, c_lo), jnp.where(more, v, hi), jnp.where(more, c, c_hi),
                done, jnp.sum(1.0 - done))

    lo, c_lo, hi, c_hi, _, _ = lax.while_loop(close_cond, close_step, state + (col(0.0), jnp.float32(R)))

    def ties_by_index():
        need = kf - c_hi
        off = jnp.zeros_like(need)
        out = []
        for x, valid_fn in segs:
            n = x.shape[-1]
            w = min(n, LANES)
            pieces = []
            for c in range(n // w):
                xc = x[:, c * w:(c + 1) * w]
                above = jnp.where(xc >= hi, 1.0, 0.0)
                tie = jnp.where(xc >= lo, 1.0, 0.0) - above
                pref = jnp.dot(tie.astype(BF16), tri_ref[:w, :w], preferred_element_type=F32)
                take = above + tie * jnp.where(off + pref <= need, 1.0, 0.0)
                pieces.append(jnp.where(valid_fn(c * w, w), jnp.where(take > 0.5, 0.0, -jnp.inf), -jnp.inf))
                off = off + pref[:, w - 1:w]
            out.append(pieces[0] if len(pieces) == 1 else jnp.concatenate(pieces, axis=-1))
        return tuple(out)

    def all_ties():
        return tuple(jnp.where(valid_fn(0, x.shape[-1]), jnp.where(x >= lo, 0.0, -jnp.inf), -jnp.inf)
                     for x, valid_fn in segs)

    return list(lax.cond(jnp.max(c_lo) > kf, ties_by_index, all_ties))


def _dsa_prompt_body(topk, tile0, q_ref, qi_ref, kwq_ref, k_ref, v_ref, kwk_ref, h_ref, wout_ref, tri_ref, out_ref):
    Tq = q_ref.shape[0]
    S = k_ref.shape[0]
    grp = N_HEADS // N_KV_HEADS
    qpos0 = (tile0 + pl.program_id(1)) * Tq

    def causal(c0, w):
        qpos = qpos0 + lax.broadcasted_iota(jnp.int32, (Tq, w), 0)
        return c0 + lax.broadcasted_iota(jnp.int32, (Tq, w), 1) <= qpos

    ki = kwk_ref[:, :IDX_DIM].astype(BF16)
    wq = kwq_ref[...]
    scores = None
    for h in range(IDX_HEADS):
        s = _dot_nt(qi_ref[:, h * IDX_DIM:(h + 1) * IDX_DIM], ki)
        t = jnp.maximum(s, 0.0) * wq[:, IDX_DIM + h:IDX_DIM + h + 1]
        scores = t if scores is None else scores + t
    scores = jnp.where(causal(0, S), scores, -jnp.inf)
    (bias,) = _topk_bias([(scores, causal)], topk, tri_ref)

    outs = []
    for g in range(N_KV_HEADS):
        ln = slice(g * HEAD_DIM, (g + 1) * HEAD_DIM)
        qs = jnp.concatenate([q_ref[:, (g * grp + hh) * HEAD_DIM:(g * grp + hh + 1) * HEAD_DIM] for hh in range(grp)],
                             axis=0)
        s = _dot_nt(qs, k_ref[:, ln])
        ps, ls = [], []
        for hh in range(grp):
            sh = s[hh * Tq:(hh + 1) * Tq] + bias
            p = jnp.exp(sh - jnp.max(sh, axis=-1, keepdims=True))
            ls.append(jnp.sum(p, axis=-1, keepdims=True))
            ps.append(p.astype(BF16))
        o = _dot(jnp.concatenate(ps, axis=0), v_ref[:, ln])
        outs += [o[hh * Tq:(hh + 1) * Tq] / ls[hh] for hh in range(grp)]
    out_ref[...] = h_ref[...] + _dot(jnp.concatenate(outs, axis=-1), wout_ref[...])


def _tri(n):
    i = jnp.arange(n)
    return (i[:, None] <= i[None, :]).astype(BF16)


def _dsa_prompt(h, q, k, v, qi, kw, w_out, q_tile):
    B, T, D = h.shape
    topk = min(TOPK_MAX, T // 4)
    wb, tri = w_out.astype(BF16), _tri(LANES)
    for tile in range(T // q_tile):
        S = (tile + 1) * q_tile
        qspec = lambda c, tile=tile: pl.BlockSpec((None, q_tile, c), lambda b, j: (b, tile + j, 0))
        kspec = lambda c, S=S: pl.BlockSpec((None, S, c), lambda b, j: (b, 0, 0))
        h = pl.pallas_call(
            functools.partial(_dsa_prompt_body, topk, tile),
            grid=(B, 1),
            in_specs=[qspec(q.shape[-1]), qspec(qi.shape[-1]), qspec(LANES), kspec(k.shape[-1]), kspec(v.shape[-1]),
                      kspec(LANES), qspec(D), _const_spec(wb.shape), _const_spec((LANES, LANES))],
            out_specs=qspec(D),
            out_shape=jax.ShapeDtypeStruct(h.shape, F32),
            input_output_aliases={6: 0},
            compiler_params=pltpu.CompilerParams(dimension_semantics=("arbitrary", "arbitrary"),
                                                 vmem_limit_bytes=VMEM_LIMIT),
            name="dsa_prompt",
        )(q, qi, kw, k, v, kw, h, wb, tri)
    return h


PAGES_PER_CHUNK = 16
Q_ROWS = SUBLANES


def _sample_index_body(layer, n_pages, pt_ref, qi_ref, wq_ref, kinew_ref, ckidx_ref, out_ref, kidx_buf, sems):
    b = pl.program_id(0)
    nb = pl.num_programs(0)
    past = n_pages * PAGE_SIZE
    n_tok = out_ref.shape[0]

    def kidx_copy(bb, p):
        return pltpu.make_async_copy(ckidx_ref.at[layer, pt_ref[bb, p]], kidx_buf.at[bb % 2, p], sems.at[bb % 2])

    def start_kidx(bb):
        for p in range(n_pages):
            kidx_copy(bb, p).start()

    @pl.when(b == 0)
    def _():
        start_kidx(b)

    for p in range(n_pages):
        kidx_copy(b, p).wait()

    @pl.when(b + 1 < nb)
    def _():
        start_kidx(b + 1)

    qi = qi_ref[...]
    wq = wq_ref[...]

    def head_sum(s):
        return jnp.sum((jnp.maximum(s, 0.0) * wq).reshape(n_tok, IDX_HEADS, s.shape[-1]), axis=1)

    for p in range(n_pages):
        out_ref[:, p * PAGE_SIZE:(p + 1) * PAGE_SIZE] = head_sum(_dot(qi, kidx_buf[b % 2, p]))
    i_new = head_sum(_dot_nt(qi, kinew_ref[:, :IDX_DIM]))
    tq = lax.broadcasted_iota(jnp.int32, i_new.shape, 0)
    out_ref[:, past:] = jnp.where(lax.broadcasted_iota(jnp.int32, i_new.shape, 1) <= tq, i_new, -jnp.inf)


def _sample_index_scores(qi, kw, cache_kidx, page_table, layer):
    B, n_tok, _ = qi.shape
    n_pages = page_table.shape[1]
    rows = n_tok * IDX_HEADS
    wq_rows = kw[:, :, IDX_DIM:IDX_DIM + IDX_HEADS].reshape(B, rows, 1)
    ki_new = jnp.pad(kw, ((0, 0), (0, LANES - n_tok), (0, 0)))
    cki = jnp.swapaxes(cache_kidx, 2, 3)
    bspec = lambda *s: pl.BlockSpec((None,) + s, lambda b, pt: (b,) + (0,) * len(s))
    grid_spec = pltpu.PrefetchScalarGridSpec(
        num_scalar_prefetch=1,
        grid=(B,),
        in_specs=[bspec(rows, IDX_DIM), bspec(rows, 1), bspec(LANES, LANES), pl.BlockSpec(memory_space=pl.ANY)],
        out_specs=bspec(n_tok, n_pages * PAGE_SIZE + LANES),
        scratch_shapes=[pltpu.VMEM((2, n_pages, IDX_DIM, PAGE_SIZE), F32), pltpu.SemaphoreType.DMA((2,))],
    )
    return pl.pallas_call(
        functools.partial(_sample_index_body, layer, n_pages),
        grid_spec=grid_spec,
        out_shape=jax.ShapeDtypeStruct((B, n_tok, n_pages * PAGE_SIZE + LANES), F32),
        compiler_params=pltpu.CompilerParams(dimension_semantics=("arbitrary",), vmem_limit_bytes=VMEM_LIMIT),
        name="sample_index_scores",
    )(page_table, qi.reshape(B, rows, IDX_DIM), wq_rows, ki_new, cki)


def _sample_topk_body(topk, n_tok, past, s_ref, tri_ref, out_ref):
    R = s_ref.shape[0]
    tq = lax.broadcasted_iota(jnp.int32, (R, LANES), 0) % n_tok

    def valid_past(c0, w):
        return lax.broadcasted_iota(jnp.int32, (R, w), 1) >= 0

    def valid_new(c0, w):
        return c0 + lax.broadcasted_iota(jnp.int32, (R, w), 1) <= tq[:, :w]

    bias_past, bias_new = _topk_bias([(s_ref[:, :past], valid_past), (s_ref[:, past:], valid_new)], topk, tri_ref)
    out_ref[:, :past] = bias_past
    out_ref[:, past:] = bias_new


def _sample_topk_bias(scores, n_tok, topk):
    R, n = scores.shape
    return pl.pallas_call(
        functools.partial(_sample_topk_body, topk, n_tok, n - LANES),
        grid=(1,),
        in_specs=[_const_spec((R, n)), _const_spec((LANES, LANES))],
        out_specs=pl.BlockSpec((R, n), lambda i: (0, 0)),
        out_shape=jax.ShapeDtypeStruct((R, n), F32),
        compiler_params=pltpu.CompilerParams(dimension_semantics=("arbitrary",), vmem_limit_bytes=VMEM_LIMIT),
        name="sample_topk_bias",
    )(scores, _tri(LANES))


def _sample_attend_body(layer, n_pages, pt_ref, q8_ref, knew_ref, vnew_ref, bias_ref, h_ref, wout_ref,
                        ck_ref, cv_ref, out_ref, kv_buf, sc_ref, sems):
    b = pl.program_id(0)
    nb = pl.num_programs(0)
    n_chunks = n_pages // PAGES_PER_CHUNK
    n_loads = 2 * n_chunks
    past = n_pages * PAGE_SIZE
    page_rows = PAGE_SIZE * N_KV_HEADS
    ch_keys = PAGES_PER_CHUNK * PAGE_SIZE
    n_tok = h_ref.shape[0]

    def kv_copy(bb, i, p):
        src = ck_ref if i < n_chunks else cv_ref
        page = pt_ref[bb, (i % n_chunks) * PAGES_PER_CHUNK + p]
        return pltpu.make_async_copy(src.at[layer, page], kv_buf.at[i % 2, pl.ds(p * page_rows, page_rows)],
                                     sems.at[i % 2])

    def start_load(bb, i):
        for p in range(PAGES_PER_CHUNK):
            kv_copy(bb, i, p).start()

    def wait_load(i):
        for p in range(PAGES_PER_CHUNK):
            kv_copy(b, i, p).wait()

    def start_after(i):
        if i + 2 < n_loads:
            start_load(b, i + 2)
        else:
            @pl.when(b + 1 < nb)
            def _():
                start_load(b + 1, i + 2 - n_loads)

    def head_rows(slot, g):
        return kv_buf[slot, pl.ds(g, ch_keys, stride=N_KV_HEADS), :]

    @pl.when(b == 0)
    def _():
        start_load(b, 0)
        start_load(b, 1)

    bias_new = bias_ref[:, past:past + Q_ROWS]

    for c in range(n_chunks):
        wait_load(c)
        ks = slice(c * ch_keys, (c + 1) * ch_keys)
        for g in range(N_KV_HEADS):
            sc_ref[g, :, ks] = _dot_nt(q8_ref[g], head_rows(c % 2, g)) + bias_ref[:, ks]
        start_after(c)

    acc, denom = [], []
    for g in range(N_KV_HEADS):
        ln = slice(g * HEAD_DIM, (g + 1) * HEAD_DIM)
        s_new = _dot_nt(q8_ref[g], knew_ref[:, ln]) + bias_new
        s_past = sc_ref[g]
        m = jnp.maximum(jnp.max(s_past, axis=-1, keepdims=True), jnp.max(s_new, axis=-1, keepdims=True))
        p_past = jnp.exp(s_past - m)
        p_new = jnp.exp(s_new - m)
        sc_ref[g] = p_past
        denom.append(jnp.sum(p_past, axis=-1, keepdims=True) + jnp.sum(p_new, axis=-1, keepdims=True))
        acc.append(_dot(p_new, vnew_ref[:, ln]))

    for c in range(n_chunks):
        i = n_chunks + c
        wait_load(i)
        ks = slice(c * ch_keys, (c + 1) * ch_keys)
        for g in range(N_KV_HEADS):
            acc[g] = acc[g] + _dot(sc_ref[g, :, ks], head_rows(i % 2, g))
        start_after(i)

    heads = []
    for g in range(N_KV_HEADS):
        og = acc[g] / denom[g]
        for hh in range(N_HEADS // N_KV_HEADS):
            heads.append(og[hh * n_tok:(hh + 1) * n_tok])
    o = jnp.concatenate(heads, axis=-1)
    out_ref[...] = h_ref[...] + _dot(o, wout_ref[...])


def _dsa_sample(h, q, k, v, qi, kw, cache_k, cache_v, cache_kidx, page_table, layer, w_out):
    B, n_tok, D = h.shape
    grp = N_HEADS // N_KV_HEADS
    assert grp * n_tok == Q_ROWS
    n_pages = page_table.shape[1]
    assert n_pages % PAGES_PER_CHUNK == 0
    past = n_pages * PAGE_SIZE
    topk = min(TOPK_MAX, (past + n_tok) // 4)
    scores = _sample_index_scores(qi, kw, cache_kidx, page_table, layer)
    bias = _sample_topk_bias(scores.reshape(B * n_tok, past + LANES), n_tok, topk).reshape(scores.shape)
    bias = jnp.concatenate([bias] * grp, axis=1)
    q8 = q.reshape(B, n_tok, N_KV_HEADS, grp, HEAD_DIM).transpose(0, 2, 3, 1, 4).reshape(B, N_KV_HEADS, Q_ROWS, HEAD_DIM)
    pad8 = lambda a: jnp.pad(a, ((0, 0), (0, Q_ROWS - n_tok), (0, 0)))
    ck = cache_k.reshape(cache_k.shape[0], cache_k.shape[1], PAGE_SIZE * N_KV_HEADS, HEAD_DIM)
    cv = cache_v.reshape(ck.shape)
    bspec = lambda *s: pl.BlockSpec((None,) + s, lambda b, pt: (b,) + (0,) * len(s))
    any_spec = pl.BlockSpec(memory_space=pl.ANY)
    grid_spec = pltpu.PrefetchScalarGridSpec(
        num_scalar_prefetch=1,
        grid=(B,),
        in_specs=[bspec(N_KV_HEADS, Q_ROWS, HEAD_DIM), bspec(Q_ROWS, k.shape[-1]), bspec(Q_ROWS, v.shape[-1]),
                  bspec(Q_ROWS, past + LANES), bspec(n_tok, D),
                  pl.BlockSpec(w_out.shape, lambda b, pt: (0, 0), pipeline_mode=pl.Buffered(1)),
                  any_spec, any_spec],
        out_specs=bspec(n_tok, D),
        scratch_shapes=[pltpu.VMEM((2, PAGES_PER_CHUNK * PAGE_SIZE * N_KV_HEADS, HEAD_DIM), F32),
                        pltpu.VMEM((N_KV_HEADS, Q_ROWS, past), F32),
                        pltpu.SemaphoreType.DMA((2,))],
    )
    return pl.pallas_call(
        functools.partial(_sample_attend_body, layer, n_pages),
        grid_spec=grid_spec,
        out_shape=jax.ShapeDtypeStruct(h.shape, F32),
        compiler_params=pltpu.CompilerParams(dimension_semantics=("arbitrary",),
                                             vmem_limit_bytes=VMEM_LIMIT),
        name="sample_attend",
    )(page_table, q8, pad8(k), pad8(v), bias, h, w_out.astype(BF16), ck, cv)


PROMPT_T_TILE = 64
PROJ_ROW_TILE = 256
PROMPT_Q_TILE = 256


def kernel(x_prompt, x_sample, cache_k, cache_v, cache_kidx, state_rec_conv, state_rec_h, state_pool, state_ffn_conv, page_table, p_prompt, p_sample, norm_mix, norm_ffn, norm_ple, w_in_rec, conv_rec_w, conv_rec_b, w_rgate, b_rgate, w_igate, b_igate, lru_lambda, w_pool, pool_scale, w_out_rec, w_in_attn, q_norm, k_norm, w_out_attn, w_up, conv_ff_w, conv_ff_b, w_down, w_ple, w_ple_gate):
    B, T, D = x_prompt.shape
    Bs, Ts, _ = x_sample.shape
    depth = norm_mix.shape[0]
    C = conv_rec_w.shape[-1]
    FF = conv_ff_w.shape[-1]
    past = page_table.shape[1] * PAGE_SIZE
    hp = x_prompt
    hs = jnp.swapaxes(x_sample, 0, 1)
    zeros = lambda *s: jnp.zeros(s, F32)
    n_attn = depth // 2
    kv_p = kv_s = None
    ffn_w = tuple(w.astype(BF16) for w in (w_up, w_down, w_ple, w_ple_gate))
    p_sample_t = jnp.swapaxes(p_sample, 1, 2)
    outs = {n: [] for n in ("rc_p", "rc_s", "rh_p", "rh_s", "pl_p", "pl_s", "ki_p", "ki_s", "fc_p", "fc_s")}
    for i in range(depth):
        j = i // 2
        if i % 2 == 0:
            wts = (norm_mix[i], w_in_rec[j], conv_rec_w[j], conv_rec_b[j], w_rgate[j], b_rgate[j],
                   w_igate[j], b_igate[j], lru_lambda[j], w_pool[j], pool_scale[j], w_out_rec[j])
            hp, c, hh, pb = _rec_layer(hp, 1, PROMPT_T_TILE, 0, zeros(B, HIST_CONV, C), zeros(B, 1, C),
                                       zeros(B, HIST_POOL, C), *wts)
            outs["rc_p"].append(_unpad_hist(c, CONV_REC - 1, 1))
            outs["rh_p"].append(hh[:, 0])
            outs["pl_p"].append(_unpad_hist(pb, POOL_BUF, 1))
            hs, c, hh, pb = _rec_layer(hs, 0, Ts, past, _pad_hist(state_rec_conv[j], HIST_CONV, 0),
                                       state_rec_h[j][None], _pad_hist(state_pool[j], HIST_POOL, 0), *wts)
            outs["rc_s"].append(_unpad_hist(c, CONV_REC - 1, 0))
            outs["rh_s"].append(hh[0])
            outs["pl_s"].append(_unpad_hist(pb, POOL_BUF, 0))
        else:
            pw = (norm_mix[i], w_in_attn[j], q_norm[j], k_norm[j])
            q, k, v, kb, vb, qi, kw = _attn_project(hp.reshape(B * T, D), jnp.arange(T), PROJ_ROW_TILE, j, n_attn,
                                                    kv_p, *pw)
            kv_p = (k, v)
            r3 = lambda a: a.reshape(B, T, -1)
            hp = _dsa_prompt(hp, r3(q), r3(kb), r3(vb), r3(qi), r3(kw), w_out_attn[j], PROMPT_Q_TILE)
            outs["ki_p"].append(r3(kw)[..., :IDX_DIM])
            pos_s = past + jnp.arange(Ts * Bs) // Bs
            q, k, v, kb, vb, qi, kw = _attn_project(hs.reshape(Ts * Bs, D), pos_s, Ts * Bs, j, n_attn, kv_s, *pw)
            kv_s = (k, v)
            bm = lambda a: jnp.swapaxes(a.reshape(Ts, Bs, -1), 0, 1)
            kw = bm(kw)
            hs = jnp.swapaxes(_dsa_sample(bm(hs), bm(q), bm(kb), bm(vb), bm(qi), kw, cache_k, cache_v, cache_kidx,
                                          page_table, j, w_out_attn[j]), 0, 1)
            outs["ki_s"].append(kw[..., :IDX_DIM])
        fw = (norm_ffn[i], ffn_w[0], conv_ff_w[i], conv_ff_b[i], ffn_w[1], norm_ple[i], ffn_w[2], ffn_w[3])
        hp, fb = _ffn_layer(hp, p_prompt, i, 1, PROMPT_T_TILE, zeros(B, HIST_FFN, FF), *fw)
        outs["fc_p"].append(_unpad_hist(fb, CONV_FF - 1, 1))
        hs, fb = _ffn_layer(hs, p_sample_t, i, 0, Ts, _pad_hist(state_ffn_conv[i], HIST_FFN, 0), *fw)
        outs["fc_s"].append(_unpad_hist(fb, CONV_FF - 1, 0))
    st = lambda n: jnp.stack(outs[n])
    kv_prompt = lambda a: a.reshape(n_attn, B, T, N_KV_HEADS, HEAD_DIM)
    kv_sample = lambda a: jnp.swapaxes(a.reshape(n_attn, Ts, Bs, N_KV_HEADS, HEAD_DIM), 1, 2)
    return (hp, jnp.swapaxes(hs, 0, 1), st("rc_p"), st("rc_s"), st("rh_p"), st("rh_s"), st("pl_p"), st("pl_s"),
            kv_prompt(kv_p[0]), kv_sample(kv_s[0]), kv_prompt(kv_p[1]), kv_sample(kv_s[1]),
            st("ki_p"), st("ki_s"), st("fc_p"), st("fc_s"))
```

```python
import functools
import math

import jax
import jax.numpy as jnp
from jax import lax
from jax.experimental import pallas as pl
from jax.experimental.pallas import tpu as pltpu

F32 = jnp.float32
BF16 = jnp.bfloat16
EPS = 1e-6
LRU_C = 8.0
POOL_WINDOWS = (2, 4, 8, 16)
REC_HEADS = 8
CONV_REC = 4
CONV_FF = 3
POOL_BUF = max(POOL_WINDOWS) - 1
N_HEADS = 8
HEAD_DIM = 128
N_KV_HEADS = 4
IDX_HEADS = 8
IDX_DIM = 64
TOPK_MAX = 256
ROPE_THETA = 10000.0
PAGE_SIZE = 128

LANES = 128
SUBLANES = 8
VMEM_LIMIT = 56 * 1024 * 1024

HIST_CONV = SUBLANES
HIST_POOL = 2 * SUBLANES
HIST_FFN = SUBLANES


def _rms(x, g):
    return x * lax.rsqrt(jnp.mean(x * x, axis=-1, keepdims=True) + EPS) * g


def _dot(a, b):
    return jnp.dot(a.astype(BF16), b.astype(BF16), preferred_element_type=F32)


def _dot_nt(a, b):
    return lax.dot_general(a.astype(BF16), b.astype(BF16), (((1,), (1,)), ((), ())),
                           preferred_element_type=F32)


def _softplus(x):
    return jnp.maximum(x, 0.0) + jnp.log1p(jnp.exp(-jnp.abs(x)))


def _expm1(x):
    u = jnp.exp(x)
    near = (u - 1.0) * x / jnp.log(u)
    return jnp.where(u == 1.0, x, jnp.where(jnp.abs(x) > 0.5, u - 1.0, near))


def _tix(ta, start, size):
    t = slice(start, start + size) if isinstance(start, int) else pl.ds(start, size)
    return (slice(None), t, slice(None)) if ta == 1 else (t, slice(None), slice(None))


def _const_spec(shape):
    nd = len(shape)
    return pl.BlockSpec(shape, lambda *_: (0,) * nd, pipeline_mode=pl.Buffered(1))


def _tile_spec(shape, ta):
    if ta == 1:
        return pl.BlockSpec(shape, lambda i: (0, i, 0))
    return pl.BlockSpec(shape, lambda i: (i, 0, 0))


def _rec_body(ta, pos0, h_ref, g_ref, win_ref, cw_ref, cb_ref, wr_ref, br_ref, wi_ref, bi_ref,
              lam_ref, wp_ref, ps_ref, wout_ref, conv0_ref, h0_ref, pool0_ref,
              out_ref, conv_out_ref, hlast_ref, pool_out_ref,
              xa_ext, xb_ext, a_s, u_s, hs_s, hc_s):
    step = pl.program_id(0)
    A, S, D = h_ref.shape
    R = A * S
    n_t = S if ta == 1 else A
    C = xa_ext.shape[-1]

    @pl.when(step == 0)
    def _():
        xa_ext[_tix(ta, 0, HIST_CONV)] = conv0_ref[...]
        xb_ext[_tix(ta, 0, HIST_POOL)] = pool0_ref[...]
        hc_s[...] = h0_ref[...]

    x = h_ref[...].reshape(R, D)
    z = _dot(_rms(x, g_ref[...]), win_ref[...])
    ga = z[:, C:2 * C]
    xa_ext[_tix(ta, HIST_CONV, n_t)] = z[:, :C].reshape(A, S, C)
    xb_ext[_tix(ta, HIST_POOL, n_t)] = z[:, 2 * C:].reshape(A, S, C)

    xc = cb_ref[...][None]
    for k in range(CONV_REC):
        xc = xc + xa_ext[_tix(ta, HIST_CONV - (CONV_REC - 1) + k, n_t)] * cw_ref[k:k + 1, :][None]
    xc = xc.reshape(R, C)

    r = jax.nn.sigmoid(_dot(xc, wr_ref[...]) + br_ref[...])
    gi = jax.nn.sigmoid(_dot(xc, wi_ref[...]) + bi_ref[...])
    log_a = (-LRU_C * r) * _softplus(-lam_ref[...])
    a_s[...] = jnp.exp(log_a).reshape(A, S, C)
    u_s[...] = (jnp.sqrt(-_expm1(2.0 * log_a)) * gi * xc).reshape(A, S, C)

    def scan_step(t, h):
        ix = _tix(ta, t, 1)
        h = a_s[ix] * h + u_s[ix]
        hs_s[ix] = h
        return h

    if n_t <= SUBLANES:
        h = hc_s[...]
        for t in range(n_t):
            h = scan_step(t, h)
    else:
        h = lax.fori_loop(0, n_t, scan_step, hc_s[...], unroll=SUBLANES)
    hc_s[...] = h
    hlast_ref[...] = h
    ya = hs_s[...].reshape(R, C) * jax.nn.gelu(ga)

    G = C // len(POOL_WINDOWS)
    pos = pos0 + step * n_t + lax.broadcasted_iota(jnp.int32, (A, S, G), ta)
    posf = pos.astype(F32) + 1.0
    ds = []
    for gidx, w in enumerate(POOL_WINDOWS):
        ln = slice(gidx * G, (gidx + 1) * G)
        cur = xb_ext[_tix(ta, HIST_POOL, n_t)[:2] + (ln,)]
        acc = cur
        for j in range(1, w):
            acc = acc + xb_ext[_tix(ta, HIST_POOL - j, n_t)[:2] + (ln,)]
        ds.append(acc / jnp.minimum(jnp.float32(w), posf) - cur)
    d = jnp.concatenate(ds, axis=-1).reshape(R, C)
    yb = _dot(d, wp_ref[...]) * ps_ref[...]

    y = _dot(ya, wout_ref[:C, :]) + _dot(yb, wout_ref[C:, :])
    out_ref[...] = (x + y).reshape(A, S, D)

    new_conv = xa_ext[_tix(ta, n_t, HIST_CONV)]
    xa_ext[_tix(ta, 0, HIST_CONV)] = new_conv
    conv_out_ref[...] = new_conv
    new_pool = xb_ext[_tix(ta, n_t, HIST_POOL)]
    xb_ext[_tix(ta, 0, HIST_POOL)] = new_pool
    pool_out_ref[...] = new_pool


def _block_diag(w):
    H, a, b = w.shape
    eye = jnp.eye(H, dtype=w.dtype)
    return (eye[:, None, :, None] * w[:, :, None, :]).reshape(H * a, H * b)


def _rec_layer(h, ta, t_tile, pos0, conv0, h0, pool0, norm_g, w_in, conv_w, conv_b,
               w_r, b_r, w_i, b_i, lam, w_pool, pool_scale, w_out):
    D = h.shape[-1]
    C = conv_w.shape[-1]
    n_time = h.shape[ta]
    n_batch = h.shape[1 - ta]
    assert n_time % t_tile == 0
    if ta == 1:
        blk = (n_batch, t_tile, D)
        ext = lambda hist: (n_batch, hist + t_tile, C)
        hist_shape = lambda hist: (n_batch, hist, C)
        row_shape = (n_batch, 1, C)
        tile_c = (n_batch, t_tile, C)
    else:
        blk = (t_tile, n_batch, D)
        ext = lambda hist: (hist + t_tile, n_batch, C)
        hist_shape = lambda hist: (hist, n_batch, C)
        row_shape = (1, n_batch, C)
        tile_c = (t_tile, n_batch, C)
    row = lambda v: v.reshape(1, -1)
    args = (h, row(norm_g), w_in.astype(BF16), conv_w, row(conv_b),
            _block_diag(w_r).astype(BF16), row(b_r), _block_diag(w_i).astype(BF16), row(b_i),
            row(lam), _block_diag(w_pool).astype(BF16), row(pool_scale), w_out.astype(BF16),
            conv0, h0, pool0)
    in_specs = [_tile_spec(blk, ta)] + [_const_spec(a.shape) for a in args[1:]]
    out_shape = (jax.ShapeDtypeStruct(h.shape, F32),
                 jax.ShapeDtypeStruct(hist_shape(HIST_CONV), F32),
                 jax.ShapeDtypeStruct(row_shape, F32),
                 jax.ShapeDtypeStruct(hist_shape(HIST_POOL), F32))
    out_specs = (_tile_spec(blk, ta),
                 pl.BlockSpec(hist_shape(HIST_CONV), lambda i: (0, 0, 0)),
                 pl.BlockSpec(row_shape, lambda i: (0, 0, 0)),
                 pl.BlockSpec(hist_shape(HIST_POOL), lambda i: (0, 0, 0)))
    scratch = [pltpu.VMEM(ext(HIST_CONV), F32), pltpu.VMEM(ext(HIST_POOL), F32),
               pltpu.VMEM(tile_c, F32), pltpu.VMEM(tile_c, F32), pltpu.VMEM(tile_c, F32),
               pltpu.VMEM(row_shape, F32)]
    return pl.pallas_call(
        functools.partial(_rec_body, ta, pos0),
        grid=(n_time // t_tile,),
        in_specs=in_specs, out_specs=out_specs, out_shape=out_shape, scratch_shapes=scratch,
        compiler_params=pltpu.CompilerParams(dimension_semantics=("arbitrary",),
                                             vmem_limit_bytes=VMEM_LIMIT),
        name="rec_pool_mixer",
    )(*args)


def _ffn_body(ta, n_chunks, h_ref, p_ref, gf_ref, wup_ref, cw_ref, cb_ref, wdn_ref, gp_ref,
              wple_ref, wpg_ref, buf0_ref, out_ref, buf_out_ref, g_ext):
    step = pl.program_id(0)
    A, S, D = h_ref.shape
    R = A * S
    n_t = S if ta == 1 else A
    FF = g_ext.shape[-1]
    ch = FF // n_chunks

    @pl.when(step == 0)
    def _():
        g_ext[_tix(ta, 0, HIST_FFN)] = buf0_ref[...]

    x = h_ref[...].reshape(R, D)
    xn = _rms(x, gf_ref[...]).astype(BF16)
    acc = jnp.zeros((R, D), F32)
    for c in range(n_chunks):
        ln = slice(c * ch, (c + 1) * ch)
        g = jnp.dot(xn, wup_ref[:, ln], preferred_element_type=F32)
        u = jnp.dot(xn, wup_ref[:, FF + c * ch:FF + (c + 1) * ch], preferred_element_type=F32)
        g_ext[_tix(ta, HIST_FFN, n_t)[:2] + (ln,)] = g.reshape(A, S, ch)
        gc = cb_ref[:, ln][None]
        for k in range(CONV_FF):
            gc = gc + (g_ext[_tix(ta, HIST_FFN - (CONV_FF - 1) + k, n_t)[:2] + (ln,)]
                       * cw_ref[k:k + 1, ln][None])
        act = jax.nn.gelu(gc).reshape(R, ch) * u
        acc = acc + _dot(act, wdn_ref[ln, :])
    h1 = x + acc
    gate = jax.nn.sigmoid(_dot(_rms(h1, gp_ref[...]), wpg_ref[...]))
    pp = _dot(p_ref[...].reshape(R, p_ref.shape[-1]), wple_ref[...])
    out_ref[...] = (h1 + pp * gate).reshape(A, S, D)

    new_buf = g_ext[_tix(ta, n_t, HIST_FFN)]
    g_ext[_tix(ta, 0, HIST_FFN)] = new_buf
    buf_out_ref[...] = new_buf


def _ffn_layer(h, p, layer, ta, t_tile, buf0, norm_f, w_up, conv_w, conv_b, w_down, norm_p, w_ple, w_pg,
               n_chunks=1):
    D = h.shape[-1]
    FF = conv_w.shape[-1]
    n_time = h.shape[ta]
    n_batch = h.shape[1 - ta]
    assert n_time % t_tile == 0 and FF % (n_chunks * LANES) == 0
    if ta == 1:
        blk = lambda c: (n_batch, t_tile, c)
        p_spec = pl.BlockSpec((None,) + blk(p.shape[-1]), lambda i: (layer, 0, i, 0))
        ext = (n_batch, HIST_FFN + t_tile, FF)
        hist = (n_batch, HIST_FFN, FF)
    else:
        blk = lambda c: (t_tile, n_batch, c)
        p_spec = pl.BlockSpec((None,) + blk(p.shape[-1]), lambda i: (layer, i, 0, 0))
        ext = (HIST_FFN + t_tile, n_batch, FF)
        hist = (HIST_FFN, n_batch, FF)
    row = lambda v: v.reshape(1, -1)
    w_spec = lambda w: pl.BlockSpec((None,) + w.shape[1:], lambda i: (layer, 0, 0), pipeline_mode=pl.Buffered(1))
    args = (h, p, row(norm_f), w_up, conv_w, row(conv_b), w_down, row(norm_p), w_ple, w_pg, buf0)
    in_specs = [_tile_spec(blk(D), ta), p_spec, _const_spec((1, D)), w_spec(w_up), _const_spec(conv_w.shape),
                _const_spec((1, FF)), w_spec(w_down), _const_spec((1, D)), w_spec(w_ple), w_spec(w_pg),
                _const_spec(buf0.shape)]
    out_shape = (jax.ShapeDtypeStruct(h.shape, F32), jax.ShapeDtypeStruct(hist, F32))
    out_specs = (_tile_spec(blk(D), ta), pl.BlockSpec(hist, lambda i: (0, 0, 0)))
    return pl.pallas_call(
        functools.partial(_ffn_body, ta, n_chunks),
        grid=(n_time // t_tile,),
        in_specs=in_specs, out_specs=out_specs, out_shape=out_shape,
        scratch_shapes=[pltpu.VMEM(ext, F32)],
        compiler_params=pltpu.CompilerParams(dimension_semantics=("arbitrary",),
                                             vmem_limit_bytes=VMEM_LIMIT),
        name="conv_ffn_ple",
    )(*args)


def _pad_hist(state, hist, ta):
    n = state.shape[1]
    if ta == 1:
        return jnp.pad(state, ((0, 0), (hist - n, 0), (0, 0)))
    return jnp.pad(jnp.swapaxes(state, 0, 1), ((hist - n, 0), (0, 0), (0, 0)))


def _unpad_hist(block, n, ta):
    if ta == 1:
        return block[:, block.shape[1] - n:, :]
    return jnp.swapaxes(block[block.shape[0] - n:], 0, 1)


Q_OFF = 0
K_OFF = N_HEADS * HEAD_DIM
V_OFF = K_OFF + N_KV_HEADS * HEAD_DIM
QI_OFF = V_OFF + N_KV_HEADS * HEAD_DIM
KW_OFF = QI_OFF + IDX_HEADS * IDX_DIM
ATTN_COLS = KW_OFF + LANES
IDX_SCALE = (IDX_DIM ** -0.5) * (IDX_HEADS ** -0.5)
ATTN_SCALE = HEAD_DIM ** -0.5


def _rope_tables(pos, dim):
    half = dim // 2
    inv = jnp.power(ROPE_THETA, -jnp.arange(half, dtype=F32) / half)
    ang = pos.astype(F32)[:, None] * inv[None, :]
    cos, sin = jnp.cos(ang), jnp.sin(ang)
    reps = LANES // dim
    return (jnp.tile(jnp.concatenate([cos, cos], axis=-1), (1, reps)),
            jnp.tile(jnp.concatenate([-sin, sin], axis=-1), (1, reps)))


def _proj_body(layer, n_stacked, x_ref, g_ref, w_ref, qn_ref, kn_ref, cq_ref, sq_ref, ci_ref, si_ref, *refs):
    q_ref, k_ref, v_ref, kb_ref, vb_ref, qi_ref, kw_ref = refs[n_stacked:]
    R = x_ref.shape[0]
    if n_stacked == 0:
        for l in range(k_ref.shape[0]):
            if l != layer:
                k_ref[l] = jnp.zeros(k_ref.shape[1:], F32)
                v_ref[l] = jnp.zeros(v_ref.shape[1:], F32)
        k_ref, v_ref = k_ref.at[layer], v_ref.at[layer]
    z = _dot(_rms(x_ref[...], g_ref[...]), w_ref[...])
    cq, sq, ci, si = cq_ref[...], sq_ref[...], ci_ref[...], si_ref[...]
    lane = lax.broadcasted_iota(jnp.int32, cq.shape, 1)
    low_half = (lane % IDX_DIM) < (IDX_DIM // 2)

    def rope_head(t):
        return t * cq + pltpu.roll(t, HEAD_DIM // 2, 1) * sq

    def rope_idx(t):
        partner = jnp.where(low_half, pltpu.roll(t, LANES - IDX_DIM // 2, 1), pltpu.roll(t, IDX_DIM // 2, 1))
        return t * ci + partner * si

    def head_norm(t, g):
        return t * lax.rsqrt(jnp.mean(t * t, axis=-1, keepdims=True) + EPS) * g

    for h in range(N_HEADS):
        ln = slice(h * HEAD_DIM, (h + 1) * HEAD_DIM)
        q_ref[:, ln] = (rope_head(head_norm(z[:, ln], qn_ref[...])) * ATTN_SCALE).astype(BF16)
    for h in range(N_KV_HEADS):
        ln = slice(h * HEAD_DIM, (h + 1) * HEAD_DIM)
        kh = rope_head(head_norm(z[:, K_OFF + h * HEAD_DIM:K_OFF + (h + 1) * HEAD_DIM], kn_ref[...]))
        vh = z[:, V_OFF + h * HEAD_DIM:V_OFF + (h + 1) * HEAD_DIM]
        k_ref[pl.ds(h, R, stride=N_KV_HEADS), :] = kh
        v_ref[pl.ds(h, R, stride=N_KV_HEADS), :] = vh
        kb_ref[:, ln] = kh.astype(BF16)
        vb_ref[:, ln] = vh.astype(BF16)
    for s in range(IDX_HEADS * IDX_DIM // LANES):
        ln = slice(s * LANES, (s + 1) * LANES)
        qi_ref[:, ln] = rope_idx(z[:, QI_OFF + s * LANES:QI_OFF + (s + 1) * LANES]).astype(BF16)
    kw = z[:, KW_OFF:]
    kw_ref[...] = jnp.where(lane < IDX_DIM, rope_idx(kw), kw * IDX_SCALE)


def _attn_project(x, pos_tab, r_tile, layer, n_layers, kv_stacked, norm_g, w_in, q_norm, k_norm):
    N, D = x.shape
    P = pos_tab.shape[0]
    assert N % r_tile == 0 and P % r_tile == 0
    w = jnp.pad(w_in, ((0, 0), (0, ATTN_COLS - w_in.shape[1]))).astype(BF16)
    cq, sq = _rope_tables(pos_tab, HEAD_DIM)
    ci, si = _rope_tables(pos_tab, IDX_DIM)
    n_tab = P // r_tile
    row_spec = lambda c: pl.BlockSpec((r_tile, c), lambda i: (i, 0))
    tab_spec = pl.BlockSpec((r_tile, LANES), lambda i: (i % n_tab, 0))
    if kv_stacked is None:
        stack_spec = pl.BlockSpec((n_layers, r_tile * N_KV_HEADS, HEAD_DIM), lambda i: (0, i, 0))
    else:
        stack_spec = pl.BlockSpec((None, r_tile * N_KV_HEADS, HEAD_DIM), lambda i: (layer, i, 0))
    stack_shape = jax.ShapeDtypeStruct((n_layers, N * N_KV_HEADS, HEAD_DIM), F32)
    row = lambda v: v.reshape(1, -1)
    kv_w = N_KV_HEADS * HEAD_DIM
    rows_out = lambda c, dt: (row_spec(c), jax.ShapeDtypeStruct((N, c), dt))
    outs = (rows_out(N_HEADS * HEAD_DIM, BF16), (stack_spec, stack_shape), (stack_spec, stack_shape),
            rows_out(kv_w, BF16), rows_out(kv_w, BF16), rows_out(IDX_HEADS * IDX_DIM, BF16), rows_out(LANES, F32))
    in_specs = [row_spec(D), _const_spec((1, D)), _const_spec(w.shape), _const_spec((1, HEAD_DIM)),
                _const_spec((1, HEAD_DIM)), tab_spec, tab_spec, tab_spec, tab_spec]
    stacked = () if kv_stacked is None else tuple(kv_stacked)
    aliases = {len(in_specs) + n: 1 + n for n in range(len(stacked))}
    in_specs += [pl.BlockSpec(memory_space=pl.ANY)] * len(stacked)
    return pl.pallas_call(
        functools.partial(_proj_body, layer, len(stacked)),
        grid=(N // r_tile,),
        in_specs=in_specs,
        out_specs=tuple(s for s, _ in outs),
        out_shape=tuple(s for _, s in outs),
        input_output_aliases=aliases,
        compiler_params=pltpu.CompilerParams(dimension_semantics=("arbitrary",),
                                             vmem_limit_bytes=VMEM_LIMIT),
        name="attn_project",
    )(x, row(norm_g), w, row(q_norm), row(k_norm), cq, sq, ci, si, *stacked)


BISECT_STEPS = 12


def _topk_bias(segs, k, tri_ref):
    kf = jnp.float32(k)
    R = segs[0][0].shape[0]

    def reduce_rows(fn, combine):
        tot = None
        for x, _ in segs:
            c = fn(x)
            tot = c if tot is None else combine(tot, c)
        return tot

    def count_ge(t):
        return reduce_rows(lambda x: jnp.sum(jnp.where(x >= t, 1.0, 0.0), axis=-1, keepdims=True), jnp.add)

    def narrow(t, lo, hi, c_hi):
        c = count_ge(t)
        up = (c >= kf) & (t > lo)
        down = (c < kf) & (t < hi)
        return jnp.where(up, t, lo), jnp.where(down, t, hi), jnp.where(down, c, c_hi)

    top = reduce_rows(lambda x: jnp.max(x, axis=-1, keepdims=True), jnp.maximum)
    bot = reduce_rows(lambda x: jnp.min(jnp.where(x > -jnp.inf, x, jnp.inf), axis=-1, keepdims=True), jnp.minimum)
    col = lambda v: jnp.full((R, 1), v, F32)
    state = narrow(bot, *narrow(top, col(-jnp.inf), col(jnp.inf), col(0.0)))
    state = lax.fori_loop(0, BISECT_STEPS, lambda _, s: narrow(s[0] * 0.5 + s[1] * 0.5, *s), state)

    def close_cond(s):
        return s[4] > 0.0

    def close_step(s):
        lo, hi, c_hi, done, _ = s
        v = reduce_rows(lambda x: jnp.max(jnp.where(x < hi, x, -jnp.inf), axis=-1, keepdims=True), jnp.maximum)
        c = count_ge(v)
        active = done < 0.5
        last = active & (c >= kf)
        more = active & (c < kf)
        done = jnp.where(last, 1.0, done)
        return (jnp.where(last, v, lo), jnp.where(more, v, hi), jnp.where(more, c, c_hi), done,
                jnp.sum(1.0 - done))

    lo, hi, c_hi, _, _ = lax.while_loop(close_cond, close_step, state + (col(0.0), jnp.float32(R)))
    need = kf - c_hi
    off = jnp.zeros_like(need)
    out = []
    for x, valid_fn in segs:
        n = x.shape[-1]
        w = min(n, LANES)
        pieces = []
        for c in range(n // w):
            xc = x[:, c * w:(c + 1) * w]
            above = jnp.where(xc >= hi, 1.0, 0.0)
            tie = jnp.where(xc >= lo, 1.0, 0.0) - above
            pref = jnp.dot(tie.astype(BF16), tri_ref[:w, :w], preferred_element_type=F32)
            take = above + tie * jnp.where(off + pref <= need, 1.0, 0.0)
            pieces.append(jnp.where(valid_fn(c * w, w), jnp.where(take > 0.5, 0.0, -jnp.inf), -jnp.inf))
            off = off + pref[:, w - 1:w]
        out.append(pieces[0] if len(pieces) == 1 else jnp.concatenate(pieces, axis=-1))
    return out


def _dsa_prompt_body(topk, tile0, q_ref, qi_ref, kwq_ref, k_ref, v_ref, kwk_ref, h_ref, wout_ref, tri_ref, out_ref):
    Tq = q_ref.shape[0]
    S = k_ref.shape[0]
    grp = N_HEADS // N_KV_HEADS
    qpos0 = (tile0 + pl.program_id(1)) * Tq

    def causal(c0, w):
        qpos = qpos0 + lax.broadcasted_iota(jnp.int32, (Tq, w), 0)
        return c0 + lax.broadcasted_iota(jnp.int32, (Tq, w), 1) <= qpos

    ki = kwk_ref[:, :IDX_DIM].astype(BF16)
    wq = kwq_ref[...]
    scores = None
    for h in range(IDX_HEADS):
        s = _dot_nt(qi_ref[:, h * IDX_DIM:(h + 1) * IDX_DIM], ki)
        t = jnp.maximum(s, 0.0) * wq[:, IDX_DIM + h:IDX_DIM + h + 1]
        scores = t if scores is None else scores + t
    scores = jnp.where(causal(0, S), scores, -jnp.inf)
    (bias,) = _topk_bias([(scores, causal)], topk, tri_ref)

    outs = []
    for g in range(N_KV_HEADS):
        ln = slice(g * HEAD_DIM, (g + 1) * HEAD_DIM)
        qs = jnp.concatenate([q_ref[:, (g * grp + hh) * HEAD_DIM:(g * grp + hh + 1) * HEAD_DIM] for hh in range(grp)],
                             axis=0)
        s = _dot_nt(qs, k_ref[:, ln])
        ps, ls = [], []
        for hh in range(grp):
            sh = s[hh * Tq:(hh + 1) * Tq] + bias
            p = jnp.exp(sh - jnp.max(sh, axis=-1, keepdims=True))
            ls.append(jnp.sum(p, axis=-1, keepdims=True))
            ps.append(p.astype(BF16))
        o = _dot(jnp.concatenate(ps, axis=0), v_ref[:, ln])
        outs += [o[hh * Tq:(hh + 1) * Tq] / ls[hh] for hh in range(grp)]
    out_ref[...] = h_ref[...] + _dot(jnp.concatenate(outs, axis=-1), wout_ref[...])


def _tri(n):
    i = jnp.arange(n)
    return (i[:, None] <= i[None, :]).astype(BF16)


def _dsa_prompt(h, q, k, v, qi, kw, w_out, q_tile):
    B, T, D = h.shape
    topk = min(TOPK_MAX, T // 4)
    wb, tri = w_out.astype(BF16), _tri(LANES)
    for tile in range(T // q_tile):
        S = (tile + 1) * q_tile
        qspec = lambda c, tile=tile: pl.BlockSpec((None, q_tile, c), lambda b, j: (b, tile + j, 0))
        kspec = lambda c, S=S: pl.BlockSpec((None, S, c), lambda b, j: (b, 0, 0))
        h = pl.pallas_call(
            functools.partial(_dsa_prompt_body, topk, tile),
            grid=(B, 1),
            in_specs=[qspec(q.shape[-1]), qspec(qi.shape[-1]), qspec(LANES), kspec(k.shape[-1]), kspec(v.shape[-1]),
                      kspec(LANES), qspec(D), _const_spec(wb.shape), _const_spec((LANES, LANES))],
            out_specs=qspec(D),
            out_shape=jax.ShapeDtypeStruct(h.shape, F32),
            input_output_aliases={6: 0},
            compiler_params=pltpu.CompilerParams(dimension_semantics=("arbitrary", "arbitrary"),
                                                 vmem_limit_bytes=VMEM_LIMIT),
            name="dsa_prompt",
        )(q, qi, kw, k, v, kw, h, wb, tri)
    return h


PAGES_PER_CHUNK = 16
Q_ROWS = SUBLANES


def _sample_index_body(layer, n_pages, pt_ref, qi_ref, wq_ref, kinew_ref, ckidx_ref, out_ref, kidx_buf, sems):
    b = pl.program_id(0)
    nb = pl.num_programs(0)
    past = n_pages * PAGE_SIZE
    n_tok = out_ref.shape[0]

    def kidx_copy(bb, p):
        return pltpu.make_async_copy(ckidx_ref.at[layer, pt_ref[bb, p]], kidx_buf.at[bb % 2, p], sems.at[bb % 2])

    def start_kidx(bb):
        for p in range(n_pages):
            kidx_copy(bb, p).start()

    @pl.when(b == 0)
    def _():
        start_kidx(b)

    for p in range(n_pages):
        kidx_copy(b, p).wait()

    @pl.when(b + 1 < nb)
    def _():
        start_kidx(b + 1)

    qi = qi_ref[...]
    wq = wq_ref[...]

    def head_sum(s):
        return jnp.sum((jnp.maximum(s, 0.0) * wq).reshape(n_tok, IDX_HEADS, s.shape[-1]), axis=1)

    for p in range(n_pages):
        out_ref[:, p * PAGE_SIZE:(p + 1) * PAGE_SIZE] = head_sum(_dot(qi, kidx_buf[b % 2, p]))
    i_new = head_sum(_dot_nt(qi, kinew_ref[:, :IDX_DIM]))
    tq = lax.broadcasted_iota(jnp.int32, i_new.shape, 0)
    out_ref[:, past:] = jnp.where(lax.broadcasted_iota(jnp.int32, i_new.shape, 1) <= tq, i_new, -jnp.inf)


def _sample_index_scores(qi, kw, cache_kidx, page_table, layer):
    B, n_tok, _ = qi.shape
    n_pages = page_table.shape[1]
    rows = n_tok * IDX_HEADS
    wq_rows = kw[:, :, IDX_DIM:IDX_DIM + IDX_HEADS].reshape(B, rows, 1)
    ki_new = jnp.pad(kw, ((0, 0), (0, LANES - n_tok), (0, 0)))
    cki = jnp.swapaxes(cache_kidx, 2, 3)
    bspec = lambda *s: pl.BlockSpec((None,) + s, lambda b, pt: (b,) + (0,) * len(s))
    grid_spec = pltpu.PrefetchScalarGridSpec(
        num_scalar_prefetch=1,
        grid=(B,),
        in_specs=[bspec(rows, IDX_DIM), bspec(rows, 1), bspec(LANES, LANES), pl.BlockSpec(memory_space=pl.ANY)],
        out_specs=bspec(n_tok, n_pages * PAGE_SIZE + LANES),
        scratch_shapes=[pltpu.VMEM((2, n_pages, IDX_DIM, PAGE_SIZE), F32), pltpu.SemaphoreType.DMA((2,))],
    )
    return pl.pallas_call(
        functools.partial(_sample_index_body, layer, n_pages),
        grid_spec=grid_spec,
        out_shape=jax.ShapeDtypeStruct((B, n_tok, n_pages * PAGE_SIZE + LANES), F32),
        compiler_params=pltpu.CompilerParams(dimension_semantics=("arbitrary",), vmem_limit_bytes=VMEM_LIMIT),
        name="sample_index_scores",
    )(page_table, qi.reshape(B, rows, IDX_DIM), wq_rows, ki_new, cki)


def _sample_topk_body(topk, n_tok, past, s_ref, tri_ref, out_ref):
    R = s_ref.shape[0]
    tq = lax.broadcasted_iota(jnp.int32, (R, LANES), 0) % n_tok

    def valid_past(c0, w):
        return lax.broadcasted_iota(jnp.int32, (R, w), 1) >= 0

    def valid_new(c0, w):
        return c0 + lax.broadcasted_iota(jnp.int32, (R, w), 1) <= tq[:, :w]

    bias_past, bias_new = _topk_bias([(s_ref[:, :past], valid_past), (s_ref[:, past:], valid_new)], topk, tri_ref)
    out_ref[:, :past] = bias_past
    out_ref[:, past:] = bias_new


def _sample_topk_bias(scores, n_tok, topk):
    R, n = scores.shape
    return pl.pallas_call(
        functools.partial(_sample_topk_body, topk, n_tok, n - LANES),
        grid=(1,),
        in_specs=[_const_spec((R, n)), _const_spec((LANES, LANES))],
        out_specs=pl.BlockSpec((R, n), lambda i: (0, 0)),
        out_shape=jax.ShapeDtypeStruct((R, n), F32),
        compiler_params=pltpu.CompilerParams(dimension_semantics=("arbitrary",), vmem_limit_bytes=VMEM_LIMIT),
        name="sample_topk_bias",
    )(scores, _tri(LANES))


def _sample_attend_body(layer, n_pages, pt_ref, q8_ref, knew_ref, vnew_ref, bias_ref, h_ref, wout_ref,
                        ck_ref, cv_ref, out_ref, kv_buf, sc_ref, sems):
    b = pl.program_id(0)
    nb = pl.num_programs(0)
    n_chunks = n_pages // PAGES_PER_CHUNK
    n_loads = 2 * n_chunks
    past = n_pages * PAGE_SIZE
    page_rows = PAGE_SIZE * N_KV_HEADS
    ch_keys = PAGES_PER_CHUNK * PAGE_SIZE
    n_tok = h_ref.shape[0]

    def kv_copy(bb, i, p):
        src = ck_ref if i < n_chunks else cv_ref
        page = pt_ref[bb, (i % n_chunks) * PAGES_PER_CHUNK + p]
        return pltpu.make_async_copy(src.at[layer, page], kv_buf.at[i % 2, pl.ds(p * page_rows, page_rows)],
                                     sems.at[i % 2])

    def start_load(bb, i):
        for p in range(PAGES_PER_CHUNK):
            kv_copy(bb, i, p).start()

    def wait_load(i):
        for p in range(PAGES_PER_CHUNK):
            kv_copy(b, i, p).wait()

    def start_after(i):
        if i + 2 < n_loads:
            start_load(b, i + 2)
        else:
            @pl.when(b + 1 < nb)
            def _():
                start_load(b + 1, i + 2 - n_loads)

    def head_rows(slot, g):
        return kv_buf[slot, pl.ds(g, ch_keys, stride=N_KV_HEADS), :]

    @pl.when(b == 0)
    def _():
        start_load(b, 0)
        start_load(b, 1)

    bias_new = bias_ref[:, past:past + Q_ROWS]

    for c in range(n_chunks):
        wait_load(c)
        ks = slice(c * ch_keys, (c + 1) * ch_keys)
        for g in range(N_KV_HEADS):
            sc_ref[g, :, ks] = _dot_nt(q8_ref[g], head_rows(c % 2, g)) + bias_ref[:, ks]
        start_after(c)

    acc, denom = [], []
    for g in range(N_KV_HEADS):
        ln = slice(g * HEAD_DIM, (g + 1) * HEAD_DIM)
        s_new = _dot_nt(q8_ref[g], knew_ref[:, ln]) + bias_new
        s_past = sc_ref[g]
        m = jnp.maximum(jnp.max(s_past, axis=-1, keepdims=True), jnp.max(s_new, axis=-1, keepdims=True))
        p_past = jnp.exp(s_past - m)
        p_new = jnp.exp(s_new - m)
        sc_ref[g] = p_past
        denom.append(jnp.sum(p_past, axis=-1, keepdims=True) + jnp.sum(p_new, axis=-1, keepdims=True))
        acc.append(_dot(p_new, vnew_ref[:, ln]))

    for c in range(n_chunks):
        i = n_chunks + c
        wait_load(i)
        ks = slice(c * ch_keys, (c + 1) * ch_keys)
        for g in range(N_KV_HEADS):
            acc[g] = acc[g] + _dot(sc_ref[g, :, ks], head_rows(i % 2, g))
        start_after(i)

    heads = []
    for g in range(N_KV_HEADS):
        og = acc[g] / denom[g]
        for hh in range(N_HEADS // N_KV_HEADS):
            heads.append(og[hh * n_tok:(hh + 1) * n_tok])
    o = jnp.concatenate(heads, axis=-1)
    out_ref[...] = h_ref[...] + _dot(o, wout_ref[...])


def _dsa_sample(h, q, k, v, qi, kw, cache_k, cache_v, cache_kidx, page_table, layer, w_out):
    B, n_tok, D = h.shape
    grp = N_HEADS // N_KV_HEADS
    assert grp * n_tok == Q_ROWS
    n_pages = page_table.shape[1]
    assert n_pages % PAGES_PER_CHUNK == 0
    past = n_pages * PAGE_SIZE
    topk = min(TOPK_MAX, (past + n_tok) // 4)
    scores = _sample_index_scores(qi, kw, cache_kidx, page_table, layer)
    bias = _sample_topk_bias(scores.reshape(B * n_tok, past + LANES), n_tok, topk).reshape(scores.shape)
    bias = jnp.concatenate([bias] * grp, axis=1)
    q8 = q.reshape(B, n_tok, N_KV_HEADS, grp, HEAD_DIM).transpose(0, 2, 3, 1, 4).reshape(B, N_KV_HEADS, Q_ROWS, HEAD_DIM)
    pad8 = lambda a: jnp.pad(a, ((0, 0), (0, Q_ROWS - n_tok), (0, 0)))
    ck = cache_k.reshape(cache_k.shape[0], cache_k.shape[1], PAGE_SIZE * N_KV_HEADS, HEAD_DIM)
    cv = cache_v.reshape(ck.shape)
    bspec = lambda *s: pl.BlockSpec((None,) + s, lambda b, pt: (b,) + (0,) * len(s))
    any_spec = pl.BlockSpec(memory_space=pl.ANY)
    grid_spec = pltpu.PrefetchScalarGridSpec(
        num_scalar_prefetch=1,
        grid=(B,),
        in_specs=[bspec(N_KV_HEADS, Q_ROWS, HEAD_DIM), bspec(Q_ROWS, k.shape[-1]), bspec(Q_ROWS, v.shape[-1]),
                  bspec(Q_ROWS, past + LANES), bspec(n_tok, D),
                  pl.BlockSpec(w_out.shape, lambda b, pt: (0, 0), pipeline_mode=pl.Buffered(1)),
                  any_spec, any_spec],
        out_specs=bspec(n_tok, D),
        scratch_shapes=[pltpu.VMEM((2, PAGES_PER_CHUNK * PAGE_SIZE * N_KV_HEADS, HEAD_DIM), F32),
                        pltpu.VMEM((N_KV_HEADS, Q_ROWS, past), F32),
                        pltpu.SemaphoreType.DMA((2,))],
    )
    return pl.pallas_call(
        functools.partial(_sample_attend_body, layer, n_pages),
        grid_spec=grid_spec,
        out_shape=jax.ShapeDtypeStruct(h.shape, F32),
        compiler_params=pltpu.CompilerParams(dimension_semantics=("arbitrary",),
                                             vmem_limit_bytes=VMEM_LIMIT),
        name="sample_attend",
    )(page_table, q8, pad8(k), pad8(v), bias, h, w_out.astype(BF16), ck, cv)


PROMPT_T_TILE = 64
REC_T_TILE = 128
PROJ_ROW_TILE = 256
PROMPT_Q_TILE = 256


def kernel(x_prompt, x_sample, cache_k, cache_v, cache_kidx, state_rec_conv, state_rec_h, state_pool, state_ffn_conv, page_table, p_prompt, p_sample, norm_mix, norm_ffn, norm_ple, w_in_rec, conv_rec_w, conv_rec_b, w_rgate, b_rgate, w_igate, b_igate, lru_lambda, w_pool, pool_scale, w_out_rec, w_in_attn, q_norm, k_norm, w_out_attn, w_up, conv_ff_w, conv_ff_b, w_down, w_ple, w_ple_gate):
    B, T, D = x_prompt.shape
    Bs, Ts, _ = x_sample.shape
    depth = norm_mix.shape[0]
    C = conv_rec_w.shape[-1]
    FF = conv_ff_w.shape[-1]
    past = page_table.shape[1] * PAGE_SIZE
    hp = x_prompt
    hs = jnp.swapaxes(x_sample, 0, 1)
    zeros = lambda *s: jnp.zeros(s, F32)
    n_attn = depth // 2
    kv_p = kv_s = None
    ffn_w = tuple(w.astype(BF16) for w in (w_up, w_down, w_ple, w_ple_gate))
    p_sample_t = jnp.swapaxes(p_sample, 1, 2)
    outs = {n: [] for n in ("rc_p", "rc_s", "rh_p", "rh_s", "pl_p", "pl_s", "ki_p", "ki_s", "fc_p", "fc_s")}
    for i in range(depth):
        j = i // 2
        if i % 2 == 0:
            wts = (norm_mix[i], w_in_rec[j], conv_rec_w[j], conv_rec_b[j], w_rgate[j], b_rgate[j],
                   w_igate[j], b_igate[j], lru_lambda[j], w_pool[j], pool_scale[j], w_out_rec[j])
            hp, c, hh, pb = _rec_layer(hp, 1, REC_T_TILE, 0, zeros(B, HIST_CONV, C), zeros(B, 1, C),
                                       zeros(B, HIST_POOL, C), *wts)
            outs["rc_p"].append(_unpad_hist(c, CONV_REC - 1, 1))
            outs["rh_p"].append(hh[:, 0])
            outs["pl_p"].append(_unpad_hist(pb, POOL_BUF, 1))
            hs, c, hh, pb = _rec_layer(hs, 0, Ts, past, _pad_hist(state_rec_conv[j], HIST_CONV, 0),
                                       state_rec_h[j][None], _pad_hist(state_pool[j], HIST_POOL, 0), *wts)
            outs["rc_s"].append(_unpad_hist(c, CONV_REC - 1, 0))
            outs["rh_s"].append(hh[0])
            outs["pl_s"].append(_unpad_hist(pb, POOL_BUF, 0))
        else:
            pw = (norm_mix[i], w_in_attn[j], q_norm[j], k_norm[j])
            q, k, v, kb, vb, qi, kw = _attn_project(hp.reshape(B * T, D), jnp.arange(T), PROJ_ROW_TILE, j, n_attn,
                                                    kv_p, *pw)
            kv_p = (k, v)
            r3 = lambda a: a.reshape(B, T, -1)
            hp = _dsa_prompt(hp, r3(q), r3(kb), r3(vb), r3(qi), r3(kw), w_out_attn[j], PROMPT_Q_TILE)
            outs["ki_p"].append(r3(kw)[..., :IDX_DIM])
            pos_s = past + jnp.arange(Ts * Bs) // Bs
            q, k, v, kb, vb, qi, kw = _attn_project(hs.reshape(Ts * Bs, D), pos_s, Ts * Bs, j, n_attn, kv_s, *pw)
            kv_s = (k, v)
            bm = lambda a: jnp.swapaxes(a.reshape(Ts, Bs, -1), 0, 1)
            kw = bm(kw)
            hs = jnp.swapaxes(_dsa_sample(bm(hs), bm(q), bm(kb), bm(vb), bm(qi), kw, cache_k, cache_v, cache_kidx,
                                          page_table, j, w_out_attn[j]), 0, 1)
            outs["ki_s"].append(kw[..., :IDX_DIM])
        fw = (norm_ffn[i], ffn_w[0], conv_ff_w[i], conv_ff_b[i], ffn_w[1], norm_ple[i], ffn_w[2], ffn_w[3])
        hp, fb = _ffn_layer(hp, p_prompt, i, 1, PROMPT_T_TILE, zeros(B, HIST_FFN, FF), *fw)
        outs["fc_p"].append(_unpad_hist(fb, CONV_FF - 1, 1))
        hs, fb = _ffn_layer(hs, p_sample_t, i, 0, Ts, _pad_hist(state_ffn_conv[i], HIST_FFN, 0), *fw)
        outs["fc_s"].append(_unpad_hist(fb, CONV_FF - 1, 0))
    st = lambda n: jnp.stack(outs[n])
    kv_prompt = lambda a: a.reshape(n_attn, B, T, N_KV_HEADS, HEAD_DIM)
    kv_sample = lambda a: jnp.swapaxes(a.reshape(n_attn, Ts, Bs, N_KV_HEADS, HEAD_DIM), 1, 2)
    return (hp, jnp.swapaxes(hs, 0, 1), st("rc_p"), st("rc_s"), st("rh_p"), st("rh_s"), st("pl_p"), st("pl_s"),
            kv_prompt(kv_p[0]), kv_sample(kv_s[0]), kv_prompt(kv_p[1]), kv_sample(kv_s[1]),
            st("ki_p"), st("ki_s"), st("fc_p"), st("fc_s"))
```
